```python
import jax, jax.numpy as jnp
from jax import lax
import numpy as np

D_MODEL = 2048
BATCH = 1
SEQ = 8192
DEPTH = 4

GRID_W = 64
D_CONV = D_MODEL // 2
CONV_A_W = 3
D_LRU = D_MODEL // 2
LRU_HEADS = 8
LRU_HD = D_LRU // LRU_HEADS
CONV_B_W = 4
RG_C = 8.0
NA_HEADS = 16
NA_HD = D_MODEL // NA_HEADS
WIN_ROWS = 8
WIN_COLS = 16
PEER_HEADS = 8
PEER_QDIM = 256
N_KEYS = 128
N_EXPERTS = N_KEYS * N_KEYS
PEER_TOPK = 16
PEER_CHUNK = 128
ALPHA = (2 * DEPTH) ** 0.25
BETA = (8 * DEPTH) ** -0.25
LN_EPS = 1e-5
N_EVEN = (DEPTH + 1) // 2
N_ODD = DEPTH // 2

kernel_name = "hybrid_conv_rglru_natten_peer_encoder"


def layer_norm(x, g, b):
    xf = x.astype(jnp.float32)
    mu = jnp.mean(xf, axis=-1, keepdims=True)
    var = jnp.mean(jnp.square(xf - mu), axis=-1, keepdims=True)
    y = (xf - mu) * lax.rsqrt(var + LN_EPS) * g.astype(jnp.float32) + b.astype(jnp.float32)
    return y.astype(x.dtype)


def depthwise_conv(x, w, pad_l, pad_r):
    return lax.conv_general_dilated(
        x, w[:, None, :], window_strides=(1,), padding=[(pad_l, pad_r)],
        dimension_numbers=("NWC", "WIO", "NWC"), feature_group_count=x.shape[-1])


def _lin_combine(left, right):
    a_l, b_l = left
    a_r, b_r = right
    return a_l * a_r, a_r * b_l + b_r


def rglru_direction(xb, conv_w, conv_b, w_r, b_r, w_i, b_i, lam, reverse):
    pad = (0, CONV_B_W - 1) if reverse else (CONV_B_W - 1, 0)
    xc = depthwise_conv(xb, conv_w, pad[0], pad[1]) + conv_b
    B, S, _ = xc.shape
    xh = xc.reshape(B, S, LRU_HEADS, LRU_HD)
    r = jax.nn.sigmoid((jnp.einsum("bshi,hij->bshj", xh, w_r).reshape(B, S, D_LRU) + b_r).astype(jnp.float32))
    i = jax.nn.sigmoid((jnp.einsum("bshi,hij->bshj", xh, w_i).reshape(B, S, D_LRU) + b_i).astype(jnp.float32))
    log_a = -RG_C * r * jax.nn.softplus(-lam.astype(jnp.float32))
    a = jnp.exp(log_a)
    u = jnp.sqrt(-jnp.expm1(2.0 * log_a)) * i * xc.astype(jnp.float32)
    _, h = lax.associative_scan(_lin_combine, (a, u), reverse=reverse, axis=1)
    return h


def short_conv_rglru_mixer(x, w_in, conv_a, conv_b_w, conv_b_b, w_r, b_r, w_i, b_i, lam, w_out):
    z = x @ w_in
    gate_b, gate_c, xa, gate_g, xb = jnp.split(
        z, [D_CONV, 2 * D_CONV, 3 * D_CONV, 3 * D_CONV + D_LRU], axis=-1)
    y_a = gate_b * depthwise_conv(gate_c * xa, conv_a, CONV_A_W // 2, CONV_A_W // 2)
    h = (rglru_direction(xb, conv_b_w[0], conv_b_b[0], w_r[0], b_r[0], w_i[0], b_i[0], lam[0], False)
         + rglru_direction(xb, conv_b_w[1], conv_b_b[1], w_r[1], b_r[1], w_i[1], b_i[1], lam[1], True))
    y_b = jax.nn.gelu(gate_g, approximate=False) * h.astype(x.dtype)
    return jnp.concatenate([y_a, y_b], axis=-1) @ w_out


def neighbourhood_attention(x, w_qkv, rel_bias, w_o):
    B, S, D = x.shape
    rows = S // GRID_W
    kh = min(WIN_ROWS, rows)
    kw = WIN_COLS
    qkv = (x @ w_qkv).reshape(B, rows, GRID_W, 3, NA_HEADS, NA_HD)
    q = qkv[:, :, :, 0] * (NA_HD ** -0.5)
    k = qkv[:, :, :, 1]
    v = qkv[:, :, :, 2]
    r = jnp.arange(rows)
    c = jnp.arange(GRID_W)
    row_start = jnp.clip(r - kh // 2, 0, rows - kh)
    row_idx = row_start[:, None] + jnp.arange(kh)[None, :]
    col_start = jnp.clip(c - kw // 2, 0, GRID_W - kw)
    col_mask = (c[None, :] >= col_start[:, None]) & (c[None, :] < col_start[:, None] + kw)
    k_band = k[:, row_idx]
    v_band = v[:, row_idx]
    roff = row_idx - r[:, None] + (WIN_ROWS - 1)
    coff = jnp.clip(c[None, :] - c[:, None], -(kw - 1), kw - 1) + (WIN_COLS - 1)
    bias = rel_bias[:, roff[:, None, :, None], coff[None, :, None, :]]
    s = jnp.einsum("brqhd,brkjhd->bhrqkj", q, k_band).astype(jnp.float32) + bias.astype(jnp.float32)[None]
    s = jnp.where(col_mask[:, None, :], s, -1e30)
    p = jax.nn.softmax(s.reshape(B, NA_HEADS, rows, GRID_W, kh * GRID_W), axis=-1)
    p = p.reshape(s.shape).astype(v.dtype)
    o = jnp.einsum("bhrqkj,brkjhd->brqhd", p, v_band).reshape(B, S, D)
    return o @ w_o


def peer(x, w_q, sub_keys, u, v):
    B, S, D = x.shape
    T = B * S
    xt = x.reshape(T, D)
    q = (xt @ w_q).reshape(T, PEER_HEADS, 2, PEER_QDIM // 2)
    s = jnp.einsum("thpd,pkd->thpk", q, sub_keys).astype(jnp.float32)
    sv, si = lax.top_k(s, PEER_TOPK)
    cand = (sv[:, :, 0, :, None] + sv[:, :, 1, None, :]).reshape(T, PEER_HEADS, PEER_TOPK * PEER_TOPK)
    cv, ci = lax.top_k(cand, PEER_TOPK)
    e = (jnp.take_along_axis(si[:, :, 0], ci // PEER_TOPK, axis=-1) * N_KEYS
         + jnp.take_along_axis(si[:, :, 1], ci % PEER_TOPK, axis=-1))
    g = jax.nn.softmax(cv, axis=-1).astype(x.dtype)

    def chunk_fn(args):
        xc, ec, gc = args
        act = jax.nn.gelu(jnp.einsum("td,thkd->thk", xc, u[ec]), approximate=False)
        return jnp.einsum("thk,thkd->td", gc * act, v[ec])

    n = T // PEER_CHUNK
    y = lax.map(chunk_fn, (xt.reshape(n, PEER_CHUNK, D),
                           e.reshape(n, PEER_CHUNK, PEER_HEADS, PEER_TOPK),
                           g.reshape(n, PEER_CHUNK, PEER_HEADS, PEER_TOPK)))
    return y.reshape(B, S, D)


def setup_inputs(seed: int = 0) -> dict:
    key = jax.random.key(seed)
    ks = jax.random.split(key, 24)

    def nrm(k, shape, scale):
        return jax.random.normal(k, shape, jnp.float32) * scale

    x = nrm(ks[0], (BATCH, SEQ, D_MODEL), 1.0)
    ev_w_in = nrm(ks[1], (N_EVEN, D_MODEL, 3 * D_CONV + 2 * D_LRU), D_MODEL ** -0.5)
    ev_conv_a = nrm(ks[2], (N_EVEN, CONV_A_W, D_CONV), CONV_A_W ** -0.5)
    ev_conv_b_w = nrm(ks[3], (N_EVEN, 2, CONV_B_W, D_LRU), CONV_B_W ** -0.5)
    ev_conv_b_b = nrm(ks[4], (N_EVEN, 2, D_LRU), 0.01)
    ev_w_r = nrm(ks[5], (N_EVEN, 2, LRU_HEADS, LRU_HD, LRU_HD), LRU_HD ** -0.5)
    ev_b_r = nrm(ks[6], (N_EVEN, 2, D_LRU), 0.01)
    ev_w_i = nrm(ks[7], (N_EVEN, 2, LRU_HEADS, LRU_HD, LRU_HD), LRU_HD ** -0.5)
    ev_b_i = nrm(ks[8], (N_EVEN, 2, D_LRU), 0.01)
    a_c = jax.random.uniform(ks[9], (N_EVEN, 2, D_LRU), jnp.float32, minval=0.9, maxval=0.999)
    a_base = a_c ** (1.0 / RG_C)
    ev_lam = jnp.log(a_base) - jnp.log1p(-a_base)
    ev_w_out = nrm(ks[10], (N_EVEN, D_CONV + D_LRU, D_MODEL), BETA * (D_CONV + D_LRU) ** -0.5)
    qkv_scale = jnp.concatenate([jnp.ones((2 * D_MODEL,), jnp.float32), jnp.full((D_MODEL,), BETA, jnp.float32)])
    od_w_qkv = nrm(ks[11], (N_ODD, D_MODEL, 3 * D_MODEL), D_MODEL ** -0.5) * qkv_scale
    od_rel_bias = nrm(ks[12], (N_ODD, NA_HEADS, 2 * WIN_ROWS - 1, 2 * WIN_COLS - 1), 0.02)
    od_w_o = nrm(ks[13], (N_ODD, D_MODEL, D_MODEL), BETA * D_MODEL ** -0.5)
    pk_w_q = nrm(ks[14], (DEPTH, D_MODEL, PEER_HEADS * PEER_QDIM), D_MODEL ** -0.5)
    pk_sub_keys = nrm(ks[15], (DEPTH, 2, N_KEYS, PEER_QDIM // 2), (PEER_QDIM // 2) ** -0.5)
    pk_u = nrm(ks[16], (DEPTH, N_EXPERTS, D_MODEL), D_MODEL ** -0.5)
    pk_v = nrm(ks[17], (DEPTH, N_EXPERTS, D_MODEL), BETA * PEER_HEADS ** -0.5)
    ln1_g = 1.0 + nrm(ks[18], (DEPTH, D_MODEL), 0.02)
    ln1_b = nrm(ks[19], (DEPTH, D_MODEL), 0.02)
    ln2_g = 1.0 + nrm(ks[20], (DEPTH, D_MODEL), 0.02)
    ln2_b = nrm(ks[21], (DEPTH, D_MODEL), 0.02)
    return {"x": x, "ev_w_in": ev_w_in, "ev_conv_a": ev_conv_a, "ev_conv_b_w": ev_conv_b_w,
            "ev_conv_b_b": ev_conv_b_b, "ev_w_r": ev_w_r, "ev_b_r": ev_b_r, "ev_w_i": ev_w_i,
            "ev_b_i": ev_b_i, "ev_lam": ev_lam, "ev_w_out": ev_w_out, "od_w_qkv": od_w_qkv,
            "od_rel_bias": od_rel_bias, "od_w_o": od_w_o, "pk_w_q": pk_w_q, "pk_sub_keys": pk_sub_keys,
            "pk_u": pk_u, "pk_v": pk_v, "ln1_g": ln1_g, "ln1_b": ln1_b, "ln2_g": ln2_g, "ln2_b": ln2_b}


def reference(x, ev_w_in, ev_conv_a, ev_conv_b_w, ev_conv_b_b, ev_w_r, ev_b_r, ev_w_i, ev_b_i,
              ev_lam, ev_w_out, od_w_qkv, od_rel_bias, od_w_o, pk_w_q, pk_sub_keys, pk_u, pk_v,
              ln1_g, ln1_b, ln2_g, ln2_b):
    for l in range(DEPTH):
        i = l // 2
        if l % 2 == 0:
            h = short_conv_rglru_mixer(x, ev_w_in[i], ev_conv_a[i], ev_conv_b_w[i], ev_conv_b_b[i],
                                       ev_w_r[i], ev_b_r[i], ev_w_i[i], ev_b_i[i], ev_lam[i], ev_w_out[i])
        else:
            h = neighbourhood_attention(x, od_w_qkv[i], od_rel_bias[i], od_w_o[i])
        x = layer_norm(ALPHA * x + h, ln1_g[l], ln1_b[l])
        x = layer_norm(ALPHA * x + peer(x, pk_w_q[l], pk_sub_keys[l], pk_u[l], pk_v[l]), ln2_g[l], ln2_b[l])
    return x
```

```python
import functools
import math

import jax
import jax.numpy as jnp
from jax import lax
from jax.experimental import pallas as pl
from jax.experimental.pallas import tpu as pltpu

F32 = jnp.float32
BF16 = jnp.bfloat16

D_MODEL = 2048
DEPTH = 4
GRID_W = 64
D_CONV = D_MODEL // 2
D_LRU = D_MODEL // 2
LRU_HEADS = 8
LRU_HD = D_LRU // LRU_HEADS
CONV_A_W = 3
CONV_B_W = 4
RG_C = 8.0
NA_HEADS = 16
NA_HD = D_MODEL // NA_HEADS
WIN_ROWS = 8
WIN_COLS = 16
PEER_HEADS = 8
PEER_QDIM = 256
N_KEYS = 128
PEER_TOPK = 16
ALPHA = (2 * DEPTH) ** 0.25
LN_EPS = 1e-5

LANES = 128
VMEM_LIMIT = 56 * 1024 * 1024
NEG_INF = float("-inf")
MASK_VALUE = -1e30

NT_DIMS = (((1,), (1,)), ((), ()))


def _params(*semantics):
    return pltpu.CompilerParams(dimension_semantics=semantics, vmem_limit_bytes=VMEM_LIMIT)


def _layer_norm(y, g, b):
    mu = jnp.mean(y, axis=-1, keepdims=True)
    yc = y - mu
    var = jnp.mean(yc * yc, axis=-1, keepdims=True)
    return yc * lax.rsqrt(var + LN_EPS) * g + b


def _gelu(x):
    return 0.5 * x * (1.0 + lax.erf(x * (1.0 / math.sqrt(2.0))))


def _matmul_kernel(x_ref, w_ref, o_ref):
    o_ref[...] = jnp.dot(x_ref[...], w_ref[...], preferred_element_type=F32).astype(o_ref.dtype)


def _matmul(x, w, out_dtype, tm=1024, tn=512):
    m, k = x.shape
    n = w.shape[1]
    tm = min(tm, m)
    return pl.pallas_call(
        _matmul_kernel,
        grid=(m // tm, n // tn),
        in_specs=[pl.BlockSpec((tm, k), lambda i, j: (i, 0)),
                  pl.BlockSpec((k, tn), lambda i, j: (0, j))],
        out_specs=pl.BlockSpec((tm, tn), lambda i, j: (i, j)),
        out_shape=jax.ShapeDtypeStruct((m, n), out_dtype),
        compiler_params=_params("parallel", "parallel"),
        name="matmul",
    )(x, w)


def _matmul_res_ln_kernel(a_ref, w_ref, x_ref, g_ref, b_ref, of_ref, ob_ref):
    h = jnp.dot(a_ref[...], w_ref[...], preferred_element_type=F32)
    out = _layer_norm(ALPHA * x_ref[...] + h, g_ref[...], b_ref[...])
    of_ref[...] = out
    ob_ref[...] = out.astype(BF16)


def _matmul_res_ln(a, w, x, g, b, tm=256):
    m, k = a.shape
    n = w.shape[1]
    return pl.pallas_call(
        _matmul_res_ln_kernel,
        grid=(m // tm,),
        in_specs=[pl.BlockSpec((tm, k), lambda i: (i, 0)),
                  pl.BlockSpec((k, n), lambda i: (0, 0)),
                  pl.BlockSpec((tm, n), lambda i: (i, 0)),
                  pl.BlockSpec((1, n), lambda i: (0, 0)),
                  pl.BlockSpec((1, n), lambda i: (0, 0))],
        out_specs=[pl.BlockSpec((tm, n), lambda i: (i, 0)),
                   pl.BlockSpec((tm, n), lambda i: (i, 0))],
        out_shape=[jax.ShapeDtypeStruct((m, n), F32), jax.ShapeDtypeStruct((m, n), BF16)],
        compiler_params=_params("parallel"),
        name="matmul_res_ln",
    )(a, w, x, g.reshape(1, n), b.reshape(1, n))


MIX_TC = 256
MIX_HALO = 16


def _mixer_kernel(gb_ref, gc_ref, xa_ref, gg_ref, xb_ref, ca_ref, cbw_ref, cbb_ref,
                  wr_ref, wi_ref, br_ref, bi_ref, lam_ref, y_ref, hf_scr):
    t_total = xb_ref.shape[0]
    n_chunks = t_total // MIX_TC
    n_ext = MIX_TC + 2 * MIX_HALO
    row = lax.broadcasted_iota(jnp.int32, (MIX_TC, LANES), 0)

    def ext(ref, c):
        t0 = pl.multiple_of(c * MIX_TC, MIX_TC)
        cur = ref[pl.ds(t0, MIX_TC), :].astype(F32)
        lo = pl.multiple_of(jnp.maximum(t0 - MIX_HALO, 0), MIX_HALO)
        hi = pl.multiple_of(jnp.minimum(t0 + MIX_TC, t_total - MIX_HALO), MIX_HALO)
        prev = jnp.where(c > 0, ref[pl.ds(lo, MIX_HALO), :].astype(F32), 0.0)
        nxt = jnp.where(c < n_chunks - 1, ref[pl.ds(hi, MIX_HALO), :].astype(F32), 0.0)
        return jnp.concatenate([prev, cur, nxt], axis=0)

    def shifted(xe, d):
        return pltpu.roll(xe, n_ext - MIX_HALO - d, 0)[:MIX_TC]

    def lru_inputs(c, direction):
        xe = ext(xb_ref, c)
        xc = jnp.zeros((MIX_TC, LANES), F32) + cbb_ref[direction:direction + 1, :]
        for k in range(CONV_B_W):
            d = k if direction == 1 else k - (CONV_B_W - 1)
            xc = xc + cbw_ref[direction, k:k + 1, :] * shifted(xe, d)
        xcb = xc.astype(BF16)
        r = jax.nn.sigmoid(jnp.dot(xcb, wr_ref[direction, 0], preferred_element_type=F32)
                           + br_ref[direction:direction + 1, :])
        i = jax.nn.sigmoid(jnp.dot(xcb, wi_ref[direction, 0], preferred_element_type=F32)
                           + bi_ref[direction:direction + 1, :])
        lam = lam_ref[direction:direction + 1, :]
        softplus_neg_lam = jnp.maximum(-lam, 0.0) + jnp.log1p(jnp.exp(-jnp.abs(lam)))
        log_a = -RG_C * r * softplus_neg_lam
        a = jnp.exp(log_a)
        u = jnp.sqrt(1.0 - jnp.exp(2.0 * log_a)) * i * xc
        return a, u

    def scan_chunk(a, u, carry, reverse):
        s = 1
        while s < MIX_TC:
            if reverse:
                keep = row < MIX_TC - s
                a_sh = jnp.where(keep, pltpu.roll(a, MIX_TC - s, 0), 1.0)
                u_sh = jnp.where(keep, pltpu.roll(u, MIX_TC - s, 0), 0.0)
            else:
                keep = row >= s
                a_sh = jnp.where(keep, pltpu.roll(a, s, 0), 1.0)
                u_sh = jnp.where(keep, pltpu.roll(u, s, 0), 0.0)
            u = a * u_sh + u
            a = a * a_sh
            s *= 2
        return u + a * carry

    def forward_body(c, carry):
        t0 = pl.multiple_of(c * MIX_TC, MIX_TC)
        pe = ext(gc_ref, c) * ext(xa_ref, c)
        conv = jnp.zeros((MIX_TC, LANES), F32)
        for k in range(CONV_A_W):
            conv = conv + ca_ref[k:k + 1, :] * shifted(pe, k - CONV_A_W // 2)
        y_a = gb_ref[pl.ds(t0, MIX_TC), :].astype(F32) * conv
        y_ref[pl.ds(t0, MIX_TC), 0:LANES] = y_a.astype(BF16)
        a, u = lru_inputs(c, 0)
        h = scan_chunk(a, u, carry, reverse=False)
        hf_scr[pl.ds(t0, MIX_TC), :] = h
        return h[MIX_TC - 1:MIX_TC, :]

    def backward_body(j, carry):
        c = n_chunks - 1 - j
        t0 = pl.multiple_of(c * MIX_TC, MIX_TC)
        a, u = lru_inputs(c, 1)
        h = scan_chunk(a, u, carry, reverse=True)
        gate = _gelu(gg_ref[pl.ds(t0, MIX_TC), :].astype(F32))
        y_b = gate * (hf_scr[pl.ds(t0, MIX_TC), :] + h)
        y_ref[pl.ds(t0, MIX_TC), LANES:2 * LANES] = y_b.astype(BF16)
        return h[0:1, :]

    zero = jnp.zeros((1, LANES), F32)
    lax.fori_loop(0, n_chunks, forward_body, zero)
    lax.fori_loop(0, n_chunks, backward_body, zero)


def _mixer(z, conv_a, conv_b_w, conv_b_b, w_r, b_r, w_i, b_i, lam):
    t = z.shape[0]
    ng = D_CONV // LANES

    def zcol(part):
        return pl.BlockSpec((t, LANES), lambda g, part=part: (0, part * ng + g))

    def vec(rows):
        return pl.BlockSpec((rows, LANES), lambda g: (0, g))

    gate_w = pl.BlockSpec((2, 1, LRU_HD, LRU_HD), lambda g: (0, g, 0, 0))
    return pl.pallas_call(
        _mixer_kernel,
        grid=(ng,),
        in_specs=[zcol(0), zcol(1), zcol(2), zcol(3), zcol(4),
                  vec(CONV_A_W),
                  pl.BlockSpec((2, CONV_B_W, LANES), lambda g: (0, 0, g)),
                  vec(2), gate_w, gate_w, vec(2), vec(2), vec(2)],
        out_specs=pl.BlockSpec((t, 2 * LANES), lambda g: (0, g)),
        out_shape=jax.ShapeDtypeStruct((t, D_MODEL), BF16),
        scratch_shapes=[pltpu.VMEM((t, LANES), F32)],
        compiler_params=_params("parallel"),
        name="mixer",
    )(z, z, z, z, z, conv_a, conv_b_w, conv_b_b, w_r.astype(BF16), w_i.astype(BF16), b_r, b_i, lam)


NA_QROWS = 8
NA_KROWS = 16


def _na_kernel(q_ref, k_ref, v_ref, bias_ref, o_ref):
    rows = k_ref.shape[0] // GRID_W
    rb = pl.program_id(1)
    kr0 = jnp.clip(rb * NA_QROWS - WIN_ROWS // 2, 0, rows - NA_KROWS)
    k0 = pl.multiple_of(kr0 * GRID_W, (WIN_ROWS // 2) * GRID_W)
    k = k_ref[pl.ds(k0, NA_KROWS * GRID_W), :]
    v = v_ref[pl.ds(k0, NA_KROWS * GRID_W), :]
    s = lax.dot_general(q_ref[...], k, NT_DIMS, preferred_element_type=F32)
    s = s * (NA_HD ** -0.5) + bias_ref[0, 0]
    m = jnp.max(s, axis=-1, keepdims=True)
    p = jnp.exp(s - m)
    l = jnp.sum(p, axis=-1, keepdims=True)
    o = jnp.dot(p.astype(BF16), v, preferred_element_type=F32)
    o_ref[...] = (o / l).astype(o_ref.dtype)


def _na_bias(rel_bias, rows):
    kh = min(WIN_ROWS, rows)
    c = jnp.arange(GRID_W)
    col_start = jnp.clip(c - WIN_COLS // 2, 0, GRID_W - WIN_COLS)
    col_ok = (c[None, :] >= col_start[:, None]) & (c[None, :] < col_start[:, None] + WIN_COLS)
    coff = jnp.clip(c[None, :] - c[:, None], -(WIN_COLS - 1), WIN_COLS - 1) + (WIN_COLS - 1)
    out = []
    for r0 in (0, min(NA_QROWS, rows - NA_QROWS), rows - NA_QROWS):
        kr0 = min(max(r0 - WIN_ROWS // 2, 0), rows - NA_KROWS)
        qr = r0 + jnp.arange(NA_QROWS)
        kr = kr0 + jnp.arange(NA_KROWS)
        row_start = jnp.clip(qr - kh // 2, 0, rows - kh)
        row_ok = (kr[None, :] >= row_start[:, None]) & (kr[None, :] < row_start[:, None] + kh)
        roff = jnp.clip(kr[None, :] - qr[:, None] + (WIN_ROWS - 1), 0, 2 * WIN_ROWS - 2)
        b = rel_bias[:, roff[:, None, :, None], coff[None, :, None, :]]
        ok = row_ok[:, None, :, None] & col_ok[None, :, None, :]
        b = jnp.where(ok[None], b, MASK_VALUE)
        out.append(b.reshape(NA_HEADS, NA_QROWS * GRID_W, NA_KROWS * GRID_W))
    return jnp.stack(out)


def _neighbourhood_attention(qkv, rel_bias):
    t = qkv.shape[0]
    rows = t // GRID_W
    nrb = rows // NA_QROWS
    tq = NA_QROWS * GRID_W
    bias = _na_bias(rel_bias, rows)

    def variant(rb):
        return jnp.where(rb == 0, 0, jnp.where(rb == nrb - 1, 2, 1))

    return pl.pallas_call(
        _na_kernel,
        grid=(NA_HEADS, nrb),
        in_specs=[pl.BlockSpec((tq, NA_HD), lambda h, rb: (rb, h)),
                  pl.BlockSpec((t, NA_HD), lambda h, rb: (0, NA_HEADS + h)),
                  pl.BlockSpec((t, NA_HD), lambda h, rb: (0, 2 * NA_HEADS + h)),
                  pl.BlockSpec((1, 1, tq, NA_KROWS * GRID_W), lambda h, rb: (variant(rb), h, 0, 0))],
        out_specs=pl.BlockSpec((tq, NA_HD), lambda h, rb: (rb, h)),
        out_shape=jax.ShapeDtypeStruct((t, D_MODEL), BF16),
        compiler_params=_params("parallel", "arbitrary"),
        name="neighbourhood_attention",
    )(qkv, qkv, qkv, bias)


SEL_TT = 256
N_TOP = PEER_TOPK + 1
CAND_PAIRS = [(p, q) for p in range(N_TOP) for q in range(N_TOP // (p + 1))]
N_CAND_ROWS = -(-len(CAND_PAIRS) // 8) * 8


def _top_values(s, n):
    rows = []
    prev = jnp.full((1, s.shape[1]), jnp.inf, F32)
    for _ in range(n):
        cur = jnp.max(jnp.where(s < prev, s, NEG_INF), axis=0, keepdims=True)
        rows.append(cur)
        prev = cur
    return rows


def _peer_select_kernel(xb_ref, wq_ref, keys_ref, b_ref, e1_ref, s2_ref, e2_ref, q_scr, cand_scr):
    tt = xb_ref.shape[0]
    q_scr[...] = jnp.dot(xb_ref[...], wq_ref[...], preferred_element_type=F32).astype(BF16)
    cand_scr[...] = jnp.full(cand_scr.shape, NEG_INF, F32)
    half = PEER_QDIM // 2

    def head_body(h, _):
        scores, tops = [], []
        for p in range(2):
            col = pl.multiple_of(h * PEER_QDIM + p * half, half)
            s = lax.dot_general(keys_ref[p], q_scr[:, pl.ds(col, half)], NT_DIMS,
                                preferred_element_type=F32)
            scores.append(s)
            tops.append(_top_values(s, N_TOP))
        a, b = tops
        for n, (p, q) in enumerate(CAND_PAIRS):
            cand_scr[n:n + 1, :] = a[p] + b[q]
        cand = cand_scr[...]
        c = _top_values(cand, N_TOP)
        c_last, c_next = c[PEER_TOPK - 1], c[PEER_TOPK]
        thr = 0.5 * (c_last + c_next)
        m = a[0] + b[0]
        z = jnp.sum(jnp.where(cand >= c_last, jnp.exp(cand - m), 0.0), axis=0, keepdims=True)
        b_ref[h] = thr - scores[0]
        e1_ref[h] = jnp.exp(scores[0] - a[0]) / z
        s2_ref[h] = scores[1]
        e2_ref[h] = jnp.exp(scores[1] - b[0])
        return 0

    lax.fori_loop(0, PEER_HEADS, head_body, 0)


def _peer_select(xb, w_q, sub_keys):
    t, d = xb.shape
    out = jax.ShapeDtypeStruct((PEER_HEADS, N_KEYS, t), F32)
    out_spec = pl.BlockSpec((PEER_HEADS, N_KEYS, SEL_TT), lambda i: (0, 0, i))
    return pl.pallas_call(
        _peer_select_kernel,
        grid=(t // SEL_TT,),
        in_specs=[pl.BlockSpec((SEL_TT, d), lambda i: (i, 0)),
                  pl.BlockSpec((d, PEER_HEADS * PEER_QDIM), lambda i: (0, 0)),
                  pl.BlockSpec((2, N_KEYS, PEER_QDIM // 2), lambda i: (0, 0, 0))],
        out_specs=[out_spec] * 4,
        out_shape=[out] * 4,
        scratch_shapes=[pltpu.VMEM((SEL_TT, PEER_HEADS * PEER_QDIM), BF16),
                        pltpu.VMEM((N_CAND_ROWS, SEL_TT), F32)],
        compiler_params=_params("parallel"),
        name="peer_select",
    )(xb, w_q, sub_keys)


PEER_TT = 512
PEER_TE = 512


def _peer_expert_kernel(x_ref, u_ref, vt_ref, b_ref, e1_ref, s2_ref, e2_ref, g_ref, beta_ref,
                        of_ref, ob_ref, xb_scr, acc_scr, a_scr):
    e = pl.program_id(1)
    n_i = PEER_TE // N_KEYS

    @pl.when(e == 0)
    def _():
        xb_scr[...] = x_ref[...].astype(BF16)
        acc_scr[...] = jnp.zeros(acc_scr.shape, F32)

    h_t = lax.dot_general(u_ref[...], xb_scr[...], NT_DIMS, preferred_element_type=F32)
    for il in range(n_i):
        i = e * n_i + il
        gate = jnp.zeros((N_KEYS, PEER_TT), F32)
        for h in range(PEER_HEADS):
            sel = s2_ref[h] >= b_ref[h, pl.ds(i, 1), :]
            gate = gate + jnp.where(sel, e2_ref[h], 0.0) * e1_ref[h, pl.ds(i, 1), :]
        act = _gelu(h_t[il * N_KEYS:(il + 1) * N_KEYS])
        a_scr[il * N_KEYS:(il + 1) * N_KEYS, :] = (act * gate).astype(BF16)
    acc_scr[...] += jnp.dot(vt_ref[...], a_scr[...], preferred_element_type=F32)

    @pl.when(e == pl.num_programs(1) - 1)
    def _():
        y = ALPHA * x_ref[...] + acc_scr[...].T
        out = _layer_norm(y, g_ref[...], beta_ref[...])
        of_ref[...] = out
        ob_ref[...] = out.astype(BF16)


def _peer_experts(x, u, vt, sel, g, b):
    t, d = x.shape
    n_e = u.shape[0]
    resident = dict(pipeline_mode=pl.Buffered(1))
    sel_spec = pl.BlockSpec((PEER_HEADS, N_KEYS, PEER_TT), lambda i, e: (0, 0, i), **resident)
    return pl.pallas_call(
        _peer_expert_kernel,
        grid=(t // PEER_TT, n_e // PEER_TE),
        in_specs=[pl.BlockSpec((PEER_TT, d), lambda i, e: (i, 0), **resident),
                  pl.BlockSpec((PEER_TE, d), lambda i, e: (e, 0)),
                  pl.BlockSpec((d, PEER_TE), lambda i, e: (0, e)),
                  sel_spec, sel_spec, sel_spec, sel_spec,
                  pl.BlockSpec((1, d), lambda i, e: (0, 0)),
                  pl.BlockSpec((1, d), lambda i, e: (0, 0))],
        out_specs=[pl.BlockSpec((PEER_TT, d), lambda i, e: (i, 0)),
                   pl.BlockSpec((PEER_TT, d), lambda i, e: (i, 0))],
        out_shape=[jax.ShapeDtypeStruct((t, d), F32), jax.ShapeDtypeStruct((t, d), BF16)],
        scratch_shapes=[pltpu.VMEM((PEER_TT, d), BF16),
                        pltpu.VMEM((d, PEER_TT), F32),
                        pltpu.VMEM((PEER_TE, PEER_TT), BF16)],
        compiler_params=_params("parallel", "arbitrary"),
        name="peer_experts",
    )(x, u, vt, *sel, g.reshape(1, d), b.reshape(1, d))


def _interleave_groups(w_out):
    ng = D_CONV // LANES
    return w_out.reshape(2, ng, LANES, w_out.shape[-1]).transpose(1, 0, 2, 3).reshape(w_out.shape)


def kernel(x, ev_w_in, ev_conv_a, ev_conv_b_w, ev_conv_b_b, ev_w_r, ev_b_r, ev_w_i, ev_b_i, ev_lam, ev_w_out, od_w_qkv, od_rel_bias, od_w_o, pk_w_q, pk_sub_keys, pk_u, pk_v, ln1_g, ln1_b, ln2_g, ln2_b):
    bsz, seq, d = x.shape
    outs = []
    for bi in range(bsz):
        xf = x[bi]
        xb = xf.astype(BF16)
        for l in range(DEPTH):
            i = l // 2
            if l % 2 == 0:
                z = _matmul(xb, ev_w_in[i].astype(BF16), BF16)
                y = _mixer(z, ev_conv_a[i], ev_conv_b_w[i], ev_conv_b_b[i], ev_w_r[i], ev_b_r[i],
                           ev_w_i[i], ev_b_i[i], ev_lam[i])
                w_o = _interleave_groups(ev_w_out[i]).astype(BF16)
            else:
                qkv = _matmul(xb, od_w_qkv[i].astype(BF16), BF16)
                y = _neighbourhood_attention(qkv, od_rel_bias[i])
                w_o = od_w_o[i].astype(BF16)
            xf, xb = _matmul_res_ln(y, w_o, xf, ln1_g[l], ln1_b[l])
            sel = _peer_select(xb, pk_w_q[l].astype(BF16), pk_sub_keys[l].astype(BF16))
            xf, xb = _peer_experts(xf, pk_u[l].astype(BF16), pk_v[l].T.astype(BF16), sel,
                                   ln2_g[l], ln2_b[l])
        outs.append(xf)
    return jnp.stack(outs)
```

```python
import math

import numpy as np
import jax
import jax.numpy as jnp
from jax import lax
from jax.experimental import pallas as pl
from jax.experimental.pallas import tpu as pltpu

F32 = jnp.float32
BF16 = jnp.bfloat16

D_MODEL = 2048
DEPTH = 4
GRID_W = 64
D_CONV = D_MODEL // 2
D_LRU = D_MODEL // 2
LRU_HEADS = 8
LRU_HD = D_LRU // LRU_HEADS
CONV_A_W = 3
CONV_B_W = 4
RG_C = 8.0
NA_HEADS = 16
NA_HD = D_MODEL // NA_HEADS
WIN_ROWS = 8
WIN_COLS = 16
PEER_HEADS = 8
PEER_QDIM = 256
N_KEYS = 128
PEER_TOPK = 16
ALPHA = (2 * DEPTH) ** 0.25
LN_EPS = 1e-5

LANES = 128
VMEM_LIMIT = 56 * 1024 * 1024
NEG_INF = float("-inf")
MASK_VALUE = -1e30

NT_DIMS = (((1,), (1,)), ((), ()))


def _params(*semantics):
    return pltpu.CompilerParams(dimension_semantics=semantics, vmem_limit_bytes=VMEM_LIMIT)


def _layer_norm(y, g, b):
    mu = jnp.mean(y, axis=-1, keepdims=True)
    yc = y - mu
    var = jnp.mean(yc * yc, axis=-1, keepdims=True)
    return yc * lax.rsqrt(var + LN_EPS) * g + b


def _gelu(x):
    return 0.5 * x * (1.0 + lax.erf(x * (1.0 / math.sqrt(2.0))))


def _matmul_kernel(x_ref, w_ref, o_ref):
    o_ref[...] = jnp.dot(x_ref[...], w_ref[...], preferred_element_type=F32).astype(o_ref.dtype)


def _matmul(x, w, out_dtype, tm=1024, tn=512):
    m, k = x.shape
    n = w.shape[1]
    tm = min(tm, m)
    return pl.pallas_call(
        _matmul_kernel,
        grid=(m // tm, n // tn),
        in_specs=[pl.BlockSpec((tm, k), lambda i, j: (i, 0)),
                  pl.BlockSpec((k, tn), lambda i, j: (0, j))],
        out_specs=pl.BlockSpec((tm, tn), lambda i, j: (i, j)),
        out_shape=jax.ShapeDtypeStruct((m, n), out_dtype),
        compiler_params=_params("parallel", "parallel"),
        name="matmul",
    )(x, w)


def _matmul_res_ln_kernel(a_ref, w_ref, x_ref, g_ref, b_ref, of_ref, ob_ref):
    h = jnp.dot(a_ref[...], w_ref[...], preferred_element_type=F32)
    out = _layer_norm(ALPHA * x_ref[...] + h, g_ref[...], b_ref[...])
    of_ref[...] = out
    ob_ref[...] = out.astype(BF16)


def _matmul_res_ln(a, w, x, g, b, tm=256):
    m, k = a.shape
    n = w.shape[1]
    return pl.pallas_call(
        _matmul_res_ln_kernel,
        grid=(m // tm,),
        in_specs=[pl.BlockSpec((tm, k), lambda i: (i, 0)),
                  pl.BlockSpec((k, n), lambda i: (0, 0)),
                  pl.BlockSpec((tm, n), lambda i: (i, 0)),
                  pl.BlockSpec((1, n), lambda i: (0, 0)),
                  pl.BlockSpec((1, n), lambda i: (0, 0))],
        out_specs=[pl.BlockSpec((tm, n), lambda i: (i, 0)),
                   pl.BlockSpec((tm, n), lambda i: (i, 0))],
        out_shape=[jax.ShapeDtypeStruct((m, n), F32), jax.ShapeDtypeStruct((m, n), BF16)],
        compiler_params=_params("parallel"),
        name="matmul_res_ln",
    )(a, w, x, g.reshape(1, n), b.reshape(1, n))


MIX_TC = 256
MIX_HALO = 16


def _mixer_kernel(gb_ref, gc_ref, xa_ref, gg_ref, xb_ref, ca_ref, cbw_ref, cbb_ref,
                  wr_ref, wi_ref, br_ref, bi_ref, lam_ref, y_ref, hf_scr):
    t_total = xb_ref.shape[0]
    n_chunks = t_total // MIX_TC
    n_ext = MIX_TC + 2 * MIX_HALO
    row = lax.broadcasted_iota(jnp.int32, (MIX_TC, LANES), 0)

    def ext(ref, c):
        t0 = pl.multiple_of(c * MIX_TC, MIX_TC)
        cur = ref[pl.ds(t0, MIX_TC), :].astype(F32)
        lo = pl.multiple_of(jnp.maximum(t0 - MIX_HALO, 0), MIX_HALO)
        hi = pl.multiple_of(jnp.minimum(t0 + MIX_TC, t_total - MIX_HALO), MIX_HALO)
        prev = jnp.where(c > 0, ref[pl.ds(lo, MIX_HALO), :].astype(F32), 0.0)
        nxt = jnp.where(c < n_chunks - 1, ref[pl.ds(hi, MIX_HALO), :].astype(F32), 0.0)
        return jnp.concatenate([prev, cur, nxt], axis=0)

    def shifted(xe, d):
        return pltpu.roll(xe, n_ext - MIX_HALO - d, 0)[:MIX_TC]

    def lru_inputs(c, direction):
        xe = ext(xb_ref, c)
        xc = jnp.zeros((MIX_TC, LANES), F32) + cbb_ref[direction:direction + 1, :]
        for k in range(CONV_B_W):
            d = k if direction == 1 else k - (CONV_B_W - 1)
            xc = xc + cbw_ref[direction, k:k + 1, :] * shifted(xe, d)
        xcb = xc.astype(BF16)
        r = jax.nn.sigmoid(jnp.dot(xcb, wr_ref[direction, 0], preferred_element_type=F32)
                           + br_ref[direction:direction + 1, :])
        i = jax.nn.sigmoid(jnp.dot(xcb, wi_ref[direction, 0], preferred_element_type=F32)
                           + bi_ref[direction:direction + 1, :])
        lam = lam_ref[direction:direction + 1, :]
        softplus_neg_lam = jnp.maximum(-lam, 0.0) + jnp.log1p(jnp.exp(-jnp.abs(lam)))
        log_a = -RG_C * r * softplus_neg_lam
        a = jnp.exp(log_a)
        u = jnp.sqrt(1.0 - jnp.exp(2.0 * log_a)) * i * xc
        return a, u

    def scan_chunk(a, u, carry, reverse):
        s = 1
        while s < MIX_TC:
            if reverse:
                keep = row < MIX_TC - s
                a_sh = jnp.where(keep, pltpu.roll(a, MIX_TC - s, 0), 1.0)
                u_sh = jnp.where(keep, pltpu.roll(u, MIX_TC - s, 0), 0.0)
            else:
                keep = row >= s
                a_sh = jnp.where(keep, pltpu.roll(a, s, 0), 1.0)
                u_sh = jnp.where(keep, pltpu.roll(u, s, 0), 0.0)
            u = a * u_sh + u
            a = a * a_sh
            s *= 2
        return u + a * carry

    def forward_body(c, carry):
        t0 = pl.multiple_of(c * MIX_TC, MIX_TC)
        pe = ext(gc_ref, c) * ext(xa_ref, c)
        conv = jnp.zeros((MIX_TC, LANES), F32)
        for k in range(CONV_A_W):
            conv = conv + ca_ref[k:k + 1, :] * shifted(pe, k - CONV_A_W // 2)
        y_a = gb_ref[pl.ds(t0, MIX_TC), :].astype(F32) * conv
        y_ref[pl.ds(t0, MIX_TC), 0:LANES] = y_a.astype(BF16)
        a, u = lru_inputs(c, 0)
        h = scan_chunk(a, u, carry, reverse=False)
        hf_scr[pl.ds(t0, MIX_TC), :] = h
        return h[MIX_TC - 1:MIX_TC, :]

    def backward_body(j, carry):
        c = n_chunks - 1 - j
        t0 = pl.multiple_of(c * MIX_TC, MIX_TC)
        a, u = lru_inputs(c, 1)
        h = scan_chunk(a, u, carry, reverse=True)
        gate = _gelu(gg_ref[pl.ds(t0, MIX_TC), :].astype(F32))
        y_b = gate * (hf_scr[pl.ds(t0, MIX_TC), :] + h)
        y_ref[pl.ds(t0, MIX_TC), LANES:2 * LANES] = y_b.astype(BF16)
        return h[0:1, :]

    zero = jnp.zeros((1, LANES), F32)
    lax.fori_loop(0, n_chunks, forward_body, zero)
    lax.fori_loop(0, n_chunks, backward_body, zero)


def _mixer(z, conv_a, conv_b_w, conv_b_b, w_r, b_r, w_i, b_i, lam):
    t = z.shape[0]
    ng = D_CONV // LANES

    def zcol(part):
        return pl.BlockSpec((t, LANES), lambda g, part=part: (0, part * ng + g))

    def vec(rows):
        return pl.BlockSpec((rows, LANES), lambda g: (0, g))

    gate_w = pl.BlockSpec((2, 1, LRU_HD, LRU_HD), lambda g: (0, g, 0, 0))
    return pl.pallas_call(
        _mixer_kernel,
        grid=(ng,),
        in_specs=[zcol(0), zcol(1), zcol(2), zcol(3), zcol(4),
                  vec(CONV_A_W),
                  pl.BlockSpec((2, CONV_B_W, LANES), lambda g: (0, 0, g)),
                  vec(2), gate_w, gate_w, vec(2), vec(2), vec(2)],
        out_specs=pl.BlockSpec((t, 2 * LANES), lambda g: (0, g)),
        out_shape=jax.ShapeDtypeStruct((t, D_MODEL), BF16),
        scratch_shapes=[pltpu.VMEM((t, LANES), F32)],
        compiler_params=_params("parallel"),
        name="mixer",
    )(z, z, z, z, z, conv_a, conv_b_w, conv_b_b, w_r.astype(BF16), w_i.astype(BF16), b_r, b_i, lam)


NA_QROWS = 8
NA_KROWS = 16


def _na_kernel(q_ref, k_ref, v_ref, bias_ref, o_ref):
    rows = k_ref.shape[0] // GRID_W
    rb = pl.program_id(1)
    kr0 = jnp.clip(rb * NA_QROWS - WIN_ROWS // 2, 0, rows - NA_KROWS)
    k0 = pl.multiple_of(kr0 * GRID_W, (WIN_ROWS // 2) * GRID_W)
    k = k_ref[pl.ds(k0, NA_KROWS * GRID_W), :]
    v = v_ref[pl.ds(k0, NA_KROWS * GRID_W), :]
    s = lax.dot_general(q_ref[...], k, NT_DIMS, preferred_element_type=F32)
    s = s * (NA_HD ** -0.5) + bias_ref[0, 0]
    m = jnp.max(s, axis=-1, keepdims=True)
    p = jnp.exp(s - m)
    l = jnp.sum(p, axis=-1, keepdims=True)
    o = jnp.dot(p.astype(BF16), v, preferred_element_type=F32)
    o_ref[...] = (o / l).astype(o_ref.dtype)


def _na_bias(rel_bias, rows):
    kh = min(WIN_ROWS, rows)
    c = np.arange(GRID_W)
    col_start = np.clip(c - WIN_COLS // 2, 0, GRID_W - WIN_COLS)
    col_ok = (c[None, :] >= col_start[:, None]) & (c[None, :] < col_start[:, None] + WIN_COLS)
    coff = np.clip(c[None, :] - c[:, None], -(WIN_COLS - 1), WIN_COLS - 1) + (WIN_COLS - 1)
    n_roff, n_coff = 2 * WIN_ROWS - 1, 2 * WIN_COLS - 1
    col_onehot = (coff[None] == np.arange(n_coff)[:, None, None]).astype(np.float32)
    by_col = jnp.einsum("hdc,cqk->hdqk", rel_bias, col_onehot, precision=lax.Precision.HIGHEST)
    out = []
    for r0 in (0, min(NA_QROWS, rows - NA_QROWS), rows - NA_QROWS):
        kr0 = min(max(r0 - WIN_ROWS // 2, 0), rows - NA_KROWS)
        qr = r0 + np.arange(NA_QROWS)
        kr = kr0 + np.arange(NA_KROWS)
        row_start = np.clip(qr - kh // 2, 0, rows - kh)
        row_ok = (kr[None, :] >= row_start[:, None]) & (kr[None, :] < row_start[:, None] + kh)
        roff = np.clip(kr[None, :] - qr[:, None] + (WIN_ROWS - 1), 0, n_roff - 1)
        row_onehot = (roff[None] == np.arange(n_roff)[:, None, None]).astype(np.float32)
        b = jnp.einsum("dij,hdqk->hiqjk", row_onehot, by_col, precision=lax.Precision.HIGHEST)
        ok = row_ok[:, None, :, None] & col_ok[None, :, None, :]
        b = jnp.where(ok[None], b, MASK_VALUE)
        out.append(b.reshape(NA_HEADS, NA_QROWS * GRID_W, NA_KROWS * GRID_W))
    return jnp.stack(out)


def _neighbourhood_attention(qkv, rel_bias):
    t = qkv.shape[0]
    rows = t // GRID_W
    nrb = rows // NA_QROWS
    tq = NA_QROWS * GRID_W
    bias = _na_bias(rel_bias, rows)

    def variant(rb):
        return jnp.where(rb == 0, 0, jnp.where(rb == nrb - 1, 2, 1))

    return pl.pallas_call(
        _na_kernel,
        grid=(NA_HEADS, nrb),
        in_specs=[pl.BlockSpec((tq, NA_HD), lambda h, rb: (rb, h)),
                  pl.BlockSpec((t, NA_HD), lambda h, rb: (0, NA_HEADS + h)),
                  pl.BlockSpec((t, NA_HD), lambda h, rb: (0, 2 * NA_HEADS + h)),
                  pl.BlockSpec((1, 1, tq, NA_KROWS * GRID_W), lambda h, rb: (variant(rb), h, 0, 0))],
        out_specs=pl.BlockSpec((tq, NA_HD), lambda h, rb: (rb, h)),
        out_shape=jax.ShapeDtypeStruct((t, D_MODEL), BF16),
        compiler_params=_params("parallel", "arbitrary"),
        name="neighbourhood_attention",
    )(qkv, qkv, qkv, bias)


SEL_TT = 256
N_TOP = PEER_TOPK + 1
CAND_PAIRS = [(p, q) for p in range(N_TOP) for q in range(N_TOP // (p + 1))]
N_CAND_ROWS = -(-len(CAND_PAIRS) // 8) * 8


def _top_values(s, n):
    rows = []
    prev = jnp.full((1, s.shape[1]), jnp.inf, F32)
    for _ in range(n):
        cur = jnp.max(jnp.where(s < prev, s, NEG_INF), axis=0, keepdims=True)
        rows.append(cur)
        prev = cur
    return rows


def _peer_select_kernel(xb_ref, wq_ref, keys_ref, cnt_ref, e1_ref, rank_ref, e2_ref, q_scr, cand_scr):
    q_scr[...] = jnp.dot(xb_ref[...], wq_ref[...], preferred_element_type=F32).astype(BF16)
    cand_scr[...] = jnp.full(cand_scr.shape, NEG_INF, F32)
    half = PEER_QDIM // 2

    def head_body(h, _):
        scores, tops = [], []
        for p in range(2):
            col = pl.multiple_of(h * PEER_QDIM + p * half, half)
            s = lax.dot_general(keys_ref[p], q_scr[:, pl.ds(col, half)], NT_DIMS,
                                preferred_element_type=F32)
            scores.append(s)
            tops.append(_top_values(s, N_TOP))
        a, b = tops
        s1, s2 = scores
        for n, (p, q) in enumerate(CAND_PAIRS):
            cand_scr[n:n + 1, :] = a[p] + b[q]
        cand = cand_scr[...]
        c = _top_values(cand, N_TOP)
        c_last, c_next = c[PEER_TOPK - 1], c[PEER_TOPK]
        thr = 0.5 * (c_last + c_next)
        m = a[0] + b[0]
        z = jnp.sum(jnp.where(cand >= c_last, jnp.exp(cand - m), 0.0), axis=0, keepdims=True)
        need = thr - s1
        cnt = jnp.zeros_like(s1)
        rank = jnp.zeros_like(s2)
        for q in range(N_TOP):
            cnt = cnt + jnp.where(b[q] >= need, 1.0, 0.0)
            rank = rank + jnp.where(b[q] > s2, 1.0, 0.0)
        cnt_ref[h] = cnt
        e1_ref[h] = jnp.exp(s1 - a[0]) / z
        rank_ref[h] = rank.astype(BF16)
        e2_ref[h] = jnp.exp(s2 - b[0]).astype(BF16)
        return 0

    lax.fori_loop(0, PEER_HEADS, head_body, 0)


def _peer_select(xb, w_q, sub_keys):
    t, d = xb.shape
    shape = (PEER_HEADS, N_KEYS, t)
    out_spec = pl.BlockSpec((PEER_HEADS, N_KEYS, SEL_TT), lambda i: (0, 0, i))
    return pl.pallas_call(
        _peer_select_kernel,
        grid=(t // SEL_TT,),
        in_specs=[pl.BlockSpec((SEL_TT, d), lambda i: (i, 0)),
                  pl.BlockSpec((d, PEER_HEADS * PEER_QDIM), lambda i: (0, 0)),
                  pl.BlockSpec((2, N_KEYS, PEER_QDIM // 2), lambda i: (0, 0, 0))],
        out_specs=[out_spec] * 4,
        out_shape=[jax.ShapeDtypeStruct(shape, F32), jax.ShapeDtypeStruct(shape, F32),
                   jax.ShapeDtypeStruct(shape, BF16), jax.ShapeDtypeStruct(shape, BF16)],
        scratch_shapes=[pltpu.VMEM((SEL_TT, PEER_HEADS * PEER_QDIM), BF16),
                        pltpu.VMEM((N_CAND_ROWS, SEL_TT), F32)],
        compiler_params=_params("parallel"),
        name="peer_select",
    )(xb, w_q, sub_keys)


PEER_TT = 512
PEER_TE = 1024
PEER_C1 = 256
PEER_C2 = 512
SUB16 = 16


def _gated_activation(h_blk, i, cnt_ref, e1_ref, rank_ref, e2_ref, a_scr, row0):
    tt = h_blk.shape[1]
    n_jb = N_KEYS // SUB16
    gates = [None] * n_jb
    for h in range(PEER_HEADS):
        cnt_i = jnp.broadcast_to(cnt_ref[h, pl.ds(i, 1), :], (SUB16, tt)).astype(BF16)
        e1_i = jnp.broadcast_to(e1_ref[h, pl.ds(i, 1), :], (SUB16, tt)).astype(BF16)
        for jb in range(n_jb):
            rows = slice(jb * SUB16, (jb + 1) * SUB16)
            term = jnp.where(rank_ref[h, rows, :] < cnt_i, e2_ref[h, rows, :], 0) * e1_i
            gates[jb] = term if h == 0 else gates[jb] + term
    for jb in range(n_jb):
        rows = slice(jb * SUB16, (jb + 1) * SUB16)
        act = _gelu(h_blk[rows])
        a_scr[row0 + jb * SUB16:row0 + (jb + 1) * SUB16, :] = act.astype(BF16) * gates[jb]


def _peer_expert_kernel(x_ref, u_ref, vt_ref, cnt_ref, e1_ref, rank_ref, e2_ref, g_ref, beta_ref,
                        of_ref, ob_ref, xb_scr, acc_scr, a_scr):
    e = pl.program_id(1)
    n_i = PEER_TE // N_KEYS

    @pl.when(e == 0)
    def _():
        xb_scr[...] = x_ref[...].astype(BF16)
        acc_scr[...] = jnp.zeros(acc_scr.shape, F32)

    xb = xb_scr[...]
    for c2 in range(PEER_TE // PEER_C2):
        for c1 in range(PEER_C2 // PEER_C1):
            r0 = c2 * PEER_C2 + c1 * PEER_C1
            h_t = lax.dot_general(u_ref[r0:r0 + PEER_C1, :], xb, NT_DIMS, preferred_element_type=F32)
            for il in range(PEER_C1 // N_KEYS):
                i = e * n_i + r0 // N_KEYS + il
                _gated_activation(h_t[il * N_KEYS:(il + 1) * N_KEYS], i, cnt_ref, e1_ref, rank_ref, e2_ref,
                                  a_scr, r0 + il * N_KEYS)
        k0 = c2 * PEER_C2
        acc_scr[...] += jnp.dot(vt_ref[:, k0:k0 + PEER_C2], a_scr[k0:k0 + PEER_C2, :],
                                preferred_element_type=F32)

    @pl.when(e == pl.num_programs(1) - 1)
    def _():
        y = ALPHA * x_ref[...] + acc_scr[...].T
        out = _layer_norm(y, g_ref[...], beta_ref[...])
        of_ref[...] = out
        ob_ref[...] = out.astype(BF16)


def _peer_experts(x, u, vt, sel, g, b):
    t, d = x.shape
    n_e = u.shape[0]
    resident = dict(pipeline_mode=pl.Buffered(1))
    sel_spec = pl.BlockSpec((PEER_HEADS, N_KEYS, PEER_TT), lambda i, e: (0, 0, i), **resident)
    return pl.pallas_call(
        _peer_expert_kernel,
        grid=(t // PEER_TT, n_e // PEER_TE),
        in_specs=[pl.BlockSpec((PEER_TT, d), lambda i, e: (i, 0), **resident),
                  pl.BlockSpec((PEER_TE, d), lambda i, e: (e, 0)),
                  pl.BlockSpec((d, PEER_TE), lambda i, e: (0, e)),
                  sel_spec, sel_spec, sel_spec, sel_spec,
                  pl.BlockSpec((1, d), lambda i, e: (0, 0)),
                  pl.BlockSpec((1, d), lambda i, e: (0, 0))],
        out_specs=[pl.BlockSpec((PEER_TT, d), lambda i, e: (i, 0)),
                   pl.BlockSpec((PEER_TT, d), lambda i, e: (i, 0))],
        out_shape=[jax.ShapeDtypeStruct((t, d), F32), jax.ShapeDtypeStruct((t, d), BF16)],
        scratch_shapes=[pltpu.VMEM((PEER_TT, d), BF16),
                        pltpu.VMEM((d, PEER_TT), F32),
                        pltpu.VMEM((PEER_TE, PEER_TT), BF16)],
        compiler_params=_params("parallel", "arbitrary"),
        name="peer_experts",
    )(x, u, vt, *sel, g.reshape(1, d), b.reshape(1, d))


def _interleave_groups(w_out):
    ng = D_CONV // LANES
    return w_out.reshape(2, ng, LANES, w_out.shape[-1]).transpose(1, 0, 2, 3).reshape(w_out.shape)


def kernel(x, ev_w_in, ev_conv_a, ev_conv_b_w, ev_conv_b_b, ev_w_r, ev_b_r, ev_w_i, ev_b_i, ev_lam, ev_w_out, od_w_qkv, od_rel_bias, od_w_o, pk_w_q, pk_sub_keys, pk_u, pk_v, ln1_g, ln1_b, ln2_g, ln2_b):
    bsz, seq, d = x.shape
    outs = []
    for bi in range(bsz):
        xf = x[bi]
        xb = xf.astype(BF16)
        for l in range(DEPTH):
            i = l // 2
            if l % 2 == 0:
                z = _matmul(xb, ev_w_in[i].astype(BF16), BF16)
                y = _mixer(z, ev_conv_a[i], ev_conv_b_w[i], ev_conv_b_b[i], ev_w_r[i], ev_b_r[i],
                           ev_w_i[i], ev_b_i[i], ev_lam[i])
                w_o = _interleave_groups(ev_w_out[i]).astype(BF16)
            else:
                qkv = _matmul(xb, od_w_qkv[i].astype(BF16), BF16)
                y = _neighbourhood_attention(qkv, od_rel_bias[i])
                w_o = od_w_o[i].astype(BF16)
            xf, xb = _matmul_res_ln(y, w_o, xf, ln1_g[l], ln1_b[l])
            sel = _peer_select(xb, pk_w_q[l].astype(BF16), pk_sub_keys[l].astype(BF16))
            xf, xb = _peer_experts(xf, pk_u[l].astype(BF16), pk_v[l].T.astype(BF16), sel,
                                   ln2_g[l], ln2_b[l])
        outs.append(xf)
    return jnp.stack(outs)
```

```python
import math

import numpy as np
import jax
import jax.numpy as jnp
from jax import lax
from jax.experimental import pallas as pl
from jax.experimental.pallas import tpu as pltpu

F32 = jnp.float32
BF16 = jnp.bfloat16

D_MODEL = 2048
DEPTH = 4
GRID_W = 64
D_CONV = D_MODEL // 2
D_LRU = D_MODEL // 2
LRU_HEADS = 8
LRU_HD = D_LRU // LRU_HEADS
CONV_A_W = 3
CONV_B_W = 4
RG_C = 8.0
NA_HEADS = 16
NA_HD = D_MODEL // NA_HEADS
WIN_ROWS = 8
WIN_COLS = 16
PEER_HEADS = 8
PEER_QDIM = 256
N_KEYS = 128
PEER_TOPK = 16
ALPHA = (2 * DEPTH) ** 0.25
LN_EPS = 1e-5

LANES = 128
VMEM_LIMIT = 56 * 1024 * 1024
NEG_INF = float("-inf")
MASK_VALUE = -1e30

NT_DIMS = (((1,), (1,)), ((), ()))


def _params(*semantics):
    return pltpu.CompilerParams(dimension_semantics=semantics, vmem_limit_bytes=VMEM_LIMIT)


def _layer_norm(y, g, b):
    mu = jnp.mean(y, axis=-1, keepdims=True)
    yc = y - mu
    var = jnp.mean(yc * yc, axis=-1, keepdims=True)
    return yc * lax.rsqrt(var + LN_EPS) * g + b


def _gelu(x):
    return 0.5 * x * (1.0 + lax.erf(x * (1.0 / math.sqrt(2.0))))


def _matmul_kernel(x_ref, w_ref, o_ref):
    o_ref[...] = jnp.dot(x_ref[...], w_ref[...], preferred_element_type=F32).astype(o_ref.dtype)


def _matmul(x, w, out_dtype, tm=1024, tn=512):
    m, k = x.shape
    n = w.shape[1]
    tm = min(tm, m)
    return pl.pallas_call(
        _matmul_kernel,
        grid=(m // tm, n // tn),
        in_specs=[pl.BlockSpec((tm, k), lambda i, j: (i, 0)),
                  pl.BlockSpec((k, tn), lambda i, j: (0, j))],
        out_specs=pl.BlockSpec((tm, tn), lambda i, j: (i, j)),
        out_shape=jax.ShapeDtypeStruct((m, n), out_dtype),
        compiler_params=_params("parallel", "parallel"),
        name="matmul",
    )(x, w)


def _matmul_res_ln_kernel(a_ref, w_ref, x_ref, g_ref, b_ref, of_ref, ob_ref):
    h = jnp.dot(a_ref[...], w_ref[...], preferred_element_type=F32)
    out = _layer_norm(ALPHA * x_ref[...] + h, g_ref[...], b_ref[...])
    of_ref[...] = out
    ob_ref[...] = out.astype(BF16)


def _matmul_res_ln(a, w, x, g, b, tm=256):
    m, k = a.shape
    n = w.shape[1]
    return pl.pallas_call(
        _matmul_res_ln_kernel,
        grid=(m // tm,),
        in_specs=[pl.BlockSpec((tm, k), lambda i: (i, 0)),
                  pl.BlockSpec((k, n), lambda i: (0, 0)),
                  pl.BlockSpec((tm, n), lambda i: (i, 0)),
                  pl.BlockSpec((1, n), lambda i: (0, 0)),
                  pl.BlockSpec((1, n), lambda i: (0, 0))],
        out_specs=[pl.BlockSpec((tm, n), lambda i: (i, 0)),
                   pl.BlockSpec((tm, n), lambda i: (i, 0))],
        out_shape=[jax.ShapeDtypeStruct((m, n), F32), jax.ShapeDtypeStruct((m, n), BF16)],
        compiler_params=_params("parallel"),
        name="matmul_res_ln",
    )(a, w, x, g.reshape(1, n), b.reshape(1, n))


MIX_TC = 256
MIX_HALO = 16


def _mixer_kernel(gb_ref, gc_ref, xa_ref, gg_ref, xb_ref, ca_ref, cbw_ref, cbb_ref,
                  wr_ref, wi_ref, br_ref, bi_ref, lam_ref, y_ref, hf_scr):
    t_total = xb_ref.shape[0]
    n_chunks = t_total // MIX_TC
    n_ext = MIX_TC + 2 * MIX_HALO
    row = lax.broadcasted_iota(jnp.int32, (MIX_TC, LANES), 0)

    def ext(ref, c):
        t0 = pl.multiple_of(c * MIX_TC, MIX_TC)
        cur = ref[pl.ds(t0, MIX_TC), :].astype(F32)
        lo = pl.multiple_of(jnp.maximum(t0 - MIX_HALO, 0), MIX_HALO)
        hi = pl.multiple_of(jnp.minimum(t0 + MIX_TC, t_total - MIX_HALO), MIX_HALO)
        prev = jnp.where(c > 0, ref[pl.ds(lo, MIX_HALO), :].astype(F32), 0.0)
        nxt = jnp.where(c < n_chunks - 1, ref[pl.ds(hi, MIX_HALO), :].astype(F32), 0.0)
        return jnp.concatenate([prev, cur, nxt], axis=0)

    def shifted(xe, d):
        return pltpu.roll(xe, n_ext - MIX_HALO - d, 0)[:MIX_TC]

    def lru_inputs(c, direction):
        xe = ext(xb_ref, c)
        xc = jnp.zeros((MIX_TC, LANES), F32) + cbb_ref[direction:direction + 1, :]
        for k in range(CONV_B_W):
            d = k if direction == 1 else k - (CONV_B_W - 1)
            xc = xc + cbw_ref[direction, k:k + 1, :] * shifted(xe, d)
        xcb = xc.astype(BF16)
        r = jax.nn.sigmoid(jnp.dot(xcb, wr_ref[direction, 0], preferred_element_type=F32)
                           + br_ref[direction:direction + 1, :])
        i = jax.nn.sigmoid(jnp.dot(xcb, wi_ref[direction, 0], preferred_element_type=F32)
                           + bi_ref[direction:direction + 1, :])
        lam = lam_ref[direction:direction + 1, :]
        softplus_neg_lam = jnp.maximum(-lam, 0.0) + jnp.log1p(jnp.exp(-jnp.abs(lam)))
        log_a = -RG_C * r * softplus_neg_lam
        a = jnp.exp(log_a)
        u = jnp.sqrt(1.0 - jnp.exp(2.0 * log_a)) * i * xc
        return a, u

    def scan_chunk(a, u, carry, reverse):
        s = 1
        while s < MIX_TC:
            if reverse:
                keep = row < MIX_TC - s
                a_sh = jnp.where(keep, pltpu.roll(a, MIX_TC - s, 0), 1.0)
                u_sh = jnp.where(keep, pltpu.roll(u, MIX_TC - s, 0), 0.0)
            else:
                keep = row >= s
                a_sh = jnp.where(keep, pltpu.roll(a, s, 0), 1.0)
                u_sh = jnp.where(keep, pltpu.roll(u, s, 0), 0.0)
            u = a * u_sh + u
            a = a * a_sh
            s *= 2
        return u + a * carry

    def forward_body(c, carry):
        t0 = pl.multiple_of(c * MIX_TC, MIX_TC)
        pe = ext(gc_ref, c) * ext(xa_ref, c)
        conv = jnp.zeros((MIX_TC, LANES), F32)
        for k in range(CONV_A_W):
            conv = conv + ca_ref[k:k + 1, :] * shifted(pe, k - CONV_A_W // 2)
        y_a = gb_ref[pl.ds(t0, MIX_TC), :].astype(F32) * conv
        y_ref[pl.ds(t0, MIX_TC), 0:LANES] = y_a.astype(BF16)
        a, u = lru_inputs(c, 0)
        h = scan_chunk(a, u, carry, reverse=False)
        hf_scr[pl.ds(t0, MIX_TC), :] = h
        return h[MIX_TC - 1:MIX_TC, :]

    def backward_body(j, carry):
        c = n_chunks - 1 - j
        t0 = pl.multiple_of(c * MIX_TC, MIX_TC)
        a, u = lru_inputs(c, 1)
        h = scan_chunk(a, u, carry, reverse=True)
        gate = _gelu(gg_ref[pl.ds(t0, MIX_TC), :].astype(F32))
        y_b = gate * (hf_scr[pl.ds(t0, MIX_TC), :] + h)
        y_ref[pl.ds(t0, MIX_TC), LANES:2 * LANES] = y_b.astype(BF16)
        return h[0:1, :]

    zero = jnp.zeros((1, LANES), F32)
    lax.fori_loop(0, n_chunks, forward_body, zero)
    lax.fori_loop(0, n_chunks, backward_body, zero)


def _mixer(z, conv_a, conv_b_w, conv_b_b, w_r, b_r, w_i, b_i, lam):
    t = z.shape[0]
    ng = D_CONV // LANES

    def zcol(part):
        return pl.BlockSpec((t, LANES), lambda g, part=part: (0, part * ng + g))

    def vec(rows):
        return pl.BlockSpec((rows, LANES), lambda g: (0, g))

    gate_w = pl.BlockSpec((2, 1, LRU_HD, LRU_HD), lambda g: (0, g, 0, 0))
    return pl.pallas_call(
        _mixer_kernel,
        grid=(ng,),
        in_specs=[zcol(0), zcol(1), zcol(2), zcol(3), zcol(4),
                  vec(CONV_A_W),
                  pl.BlockSpec((2, CONV_B_W, LANES), lambda g: (0, 0, g)),
                  vec(2), gate_w, gate_w, vec(2), vec(2), vec(2)],
        out_specs=pl.BlockSpec((t, 2 * LANES), lambda g: (0, g)),
        out_shape=jax.ShapeDtypeStruct((t, D_MODEL), BF16),
        scratch_shapes=[pltpu.VMEM((t, LANES), F32)],
        compiler_params=_params("parallel"),
        name="mixer",
    )(z, z, z, z, z, conv_a, conv_b_w, conv_b_b, w_r.astype(BF16), w_i.astype(BF16), b_r, b_i, lam)


NA_QROWS = 8
NA_KROWS = 16


def _na_block_plan(rows, r0):
    kh = min(WIN_ROWS, rows)
    kr0 = min(max(r0 - WIN_ROWS // 2, 0), rows - NA_KROWS)
    plan = []
    for qi in range(NA_QROWS):
        qr = r0 + qi
        row_start = min(max(qr - kh // 2, 0), rows - kh)
        for pair in range(NA_KROWS // 2):
            kr = (kr0 + 2 * pair, kr0 + 2 * pair + 1)
            ok = [row_start <= r < row_start + kh for r in kr]
            d = [r - qr + (WIN_ROWS - 1) for r in kr]
            if ok[0] and ok[1]:
                plan.append((qi, pair, 0, d[0]))
            elif ok[0]:
                plan.append((qi, pair, 1, d[0]))
            elif ok[1]:
                plan.append((qi, pair, 2, d[1]))
    return plan


def _na_kernel(q_ref, k_ref, v_ref, tab_ref, o_ref, bias_scr):
    rows = k_ref.shape[0] // GRID_W
    nrb = rows // NA_QROWS
    rb = pl.program_id(1)

    def build(r0):
        bias_scr[...] = jnp.full(bias_scr.shape, MASK_VALUE, F32)
        for qi, pair, kind, d in _na_block_plan(rows, r0):
            bias_scr[qi * GRID_W:(qi + 1) * GRID_W, pair * LANES:(pair + 1) * LANES] = tab_ref[0, kind, d]

    pl.when(rb == 0)(lambda: build(0))
    if nrb > 2:
        pl.when(rb == 1)(lambda: build(NA_QROWS))
    pl.when(rb == nrb - 1)(lambda: build(rows - NA_QROWS))

    kr0 = jnp.clip(rb * NA_QROWS - WIN_ROWS // 2, 0, rows - NA_KROWS)
    k0 = pl.multiple_of(kr0 * GRID_W, (WIN_ROWS // 2) * GRID_W)
    k = k_ref[pl.ds(k0, NA_KROWS * GRID_W), :]
    v = v_ref[pl.ds(k0, NA_KROWS * GRID_W), :]
    s = lax.dot_general(q_ref[...], k, NT_DIMS, preferred_element_type=F32)
    s = s * (NA_HD ** -0.5) + bias_scr[...]
    m = jnp.max(s, axis=-1, keepdims=True)
    p = jnp.exp(s - m)
    l = jnp.sum(p, axis=-1, keepdims=True)
    o = jnp.dot(p.astype(BF16), v, preferred_element_type=F32)
    o_ref[...] = (o / l).astype(o_ref.dtype)


def _na_bias_table(rel_bias):
    c = np.arange(GRID_W)
    col_start = np.clip(c - WIN_COLS // 2, 0, GRID_W - WIN_COLS)
    col_ok = (c[None, :] >= col_start[:, None]) & (c[None, :] < col_start[:, None] + WIN_COLS)
    coff = np.clip(c[None, :] - c[:, None], -(WIN_COLS - 1), WIN_COLS - 1) + (WIN_COLS - 1)
    n_coff = 2 * WIN_COLS - 1
    col_onehot = (coff[None] == np.arange(n_coff)[:, None, None]).astype(np.float32)
    by_col = jnp.einsum("hdc,cqk->hdqk", rel_bias, col_onehot, precision=lax.Precision.HIGHEST)
    e = jnp.where(col_ok[None, None], by_col, MASK_VALUE)
    masked = jnp.full_like(e, MASK_VALUE)
    e_next = jnp.concatenate([e[:, 1:], masked[:, :1]], axis=1)
    return jnp.stack([jnp.concatenate([e, e_next], axis=-1),
                      jnp.concatenate([e, masked], axis=-1),
                      jnp.concatenate([masked, e], axis=-1)], axis=1)


def _neighbourhood_attention(qkv, rel_bias):
    t = qkv.shape[0]
    rows = t // GRID_W
    nrb = rows // NA_QROWS
    tq = NA_QROWS * GRID_W
    assert 2 * GRID_W == LANES and rows >= NA_KROWS
    table = _na_bias_table(rel_bias)
    return pl.pallas_call(
        _na_kernel,
        grid=(NA_HEADS, nrb),
        in_specs=[pl.BlockSpec((tq, NA_HD), lambda h, rb: (rb, h)),
                  pl.BlockSpec((t, NA_HD), lambda h, rb: (0, NA_HEADS + h)),
                  pl.BlockSpec((t, NA_HD), lambda h, rb: (0, 2 * NA_HEADS + h)),
                  pl.BlockSpec((1,) + table.shape[1:], lambda h, rb: (h, 0, 0, 0, 0))],
        out_specs=pl.BlockSpec((tq, NA_HD), lambda h, rb: (rb, h)),
        out_shape=jax.ShapeDtypeStruct((t, D_MODEL), BF16),
        scratch_shapes=[pltpu.VMEM((tq, NA_KROWS * GRID_W), F32)],
        compiler_params=_params("parallel", "arbitrary"),
        name="neighbourhood_attention",
    )(qkv, qkv, qkv, table)


SEL_TT = 256
N_TOP = PEER_TOPK + 1
CAND_PAIRS = [(p, q) for p in range(N_TOP) for q in range(N_TOP // (p + 1))]
N_CAND_ROWS = -(-len(CAND_PAIRS) // 8) * 8


def _top_values(s, n):
    rows = []
    prev = jnp.full((1, s.shape[1]), jnp.inf, F32)
    for _ in range(n):
        cur = jnp.max(jnp.where(s < prev, s, NEG_INF), axis=0, keepdims=True)
        rows.append(cur)
        prev = cur
    return rows


def _peer_select_kernel(xb_ref, wq_ref, keys_ref, cnt_ref, e1_ref, rank_ref, e2_ref, q_scr, cand_scr):
    q_scr[...] = jnp.dot(xb_ref[...], wq_ref[...], preferred_element_type=F32).astype(BF16)
    cand_scr[...] = jnp.full(cand_scr.shape, NEG_INF, F32)
    half = PEER_QDIM // 2

    def head_body(h, _):
        scores, tops = [], []
        for p in range(2):
            col = pl.multiple_of(h * PEER_QDIM + p * half, half)
            s = lax.dot_general(keys_ref[p], q_scr[:, pl.ds(col, half)], NT_DIMS,
                                preferred_element_type=F32)
            scores.append(s)
            tops.append(_top_values(s, N_TOP))
        a, b = tops
        s1, s2 = scores
        for n, (p, q) in enumerate(CAND_PAIRS):
            cand_scr[n:n + 1, :] = a[p] + b[q]
        cand = cand_scr[...]
        c = _top_values(cand, N_TOP)
        c_last, c_next = c[PEER_TOPK - 1], c[PEER_TOPK]
        thr = 0.5 * (c_last + c_next)
        m = a[0] + b[0]
        z = jnp.sum(jnp.where(cand >= c_last, jnp.exp(cand - m), 0.0), axis=0, keepdims=True)
        need = thr - s1
        cnt = jnp.zeros_like(s1)
        rank = jnp.zeros_like(s2)
        for q in range(N_TOP):
            cnt = jnp.where(b[q] >= need, q + 1.0, cnt)
            rank = jnp.where(b[q] > s2, q + 1.0, rank)
        cnt_ref[h] = cnt
        e1_ref[h] = jnp.exp(s1 - a[0]) / z
        rank_ref[h] = rank.astype(BF16)
        e2_ref[h] = jnp.exp(s2 - b[0]).astype(BF16)
        return 0

    lax.fori_loop(0, PEER_HEADS, head_body, 0)


def _peer_select(xb, w_q, sub_keys):
    t, d = xb.shape
    shape = (PEER_HEADS, N_KEYS, t)
    out_spec = pl.BlockSpec((PEER_HEADS, N_KEYS, SEL_TT), lambda i: (0, 0, i))
    return pl.pallas_call(
        _peer_select_kernel,
        grid=(t // SEL_TT,),
        in_specs=[pl.BlockSpec((SEL_TT, d), lambda i: (i, 0)),
                  pl.BlockSpec((d, PEER_HEADS * PEER_QDIM), lambda i: (0, 0)),
                  pl.BlockSpec((2, N_KEYS, PEER_QDIM // 2), lambda i: (0, 0, 0))],
        out_specs=[out_spec] * 4,
        out_shape=[jax.ShapeDtypeStruct(shape, F32), jax.ShapeDtypeStruct(shape, F32),
                   jax.ShapeDtypeStruct(shape, BF16), jax.ShapeDtypeStruct(shape, BF16)],
        scratch_shapes=[pltpu.VMEM((SEL_TT, PEER_HEADS * PEER_QDIM), BF16),
                        pltpu.VMEM((N_CAND_ROWS, SEL_TT), F32)],
        compiler_params=_params("parallel"),
        name="peer_select",
    )(xb, w_q, sub_keys)


PEER_TT = 512
PEER_TE = 1024
PEER_C1 = 256
PEER_C2 = 512
PEER_D2 = 512
SUB16 = 16
assert PEER_TE == 4 * PEER_C1 == 2 * PEER_C2 and D_MODEL == 4 * PEER_D2


def _gated_activation(h_blk, i, cnt_ref, e1_ref, rank_ref, e2_ref, a_scr, row0):
    tt = h_blk.shape[1]
    n_jb = N_KEYS // SUB16
    gates = [None] * n_jb
    for h in range(PEER_HEADS):
        cnt_i = jnp.broadcast_to(cnt_ref[h, pl.ds(i, 1), :], (SUB16, tt)).astype(BF16)
        e1_i = jnp.broadcast_to(e1_ref[h, pl.ds(i, 1), :], (SUB16, tt)).astype(BF16)
        for jb in range(n_jb):
            rows = slice(jb * SUB16, (jb + 1) * SUB16)
            term = jnp.where(rank_ref[h, rows, :] < cnt_i, e2_ref[h, rows, :], 0) * e1_i
            gates[jb] = term if h == 0 else gates[jb] + term
    for jb in range(n_jb):
        rows = slice(jb * SUB16, (jb + 1) * SUB16)
        act = _gelu(h_blk[rows])
        a_scr[row0 + jb * SUB16:row0 + (jb + 1) * SUB16, :] = act.astype(BF16) * gates[jb]


def _peer_expert_kernel(x_ref, u_ref, vt_ref, cnt_ref, e1_ref, rank_ref, e2_ref, g_ref, beta_ref,
                        of_ref, ob_ref, xb_scr, acc_scr, a_scr):
    e = pl.program_id(1)
    n_i = PEER_TE // N_KEYS

    @pl.when(e == 0)
    def _():
        xb_scr[...] = x_ref[...].astype(BF16)
        acc_scr[...] = jnp.zeros(acc_scr.shape, F32)

    xb = xb_scr[...]

    def first_matmul(c):
        r0 = c * PEER_C1
        return lax.dot_general(u_ref[r0:r0 + PEER_C1, :], xb, NT_DIMS, preferred_element_type=F32)

    def activation(ii, h_t):
        r = (ii * N_KEYS) % PEER_C1
        _gated_activation(h_t[r:r + N_KEYS], e * n_i + ii, cnt_ref, e1_ref, rank_ref, e2_ref, a_scr, ii * N_KEYS)

    def second_matmul(c2, piece):
        r0, k0 = piece * PEER_D2, c2 * PEER_C2
        acc_scr[r0:r0 + PEER_D2, :] += jnp.dot(vt_ref[r0:r0 + PEER_D2, k0:k0 + PEER_C2], a_scr[k0:k0 + PEER_C2, :],
                                               preferred_element_type=F32)

    h0 = first_matmul(0)
    h1 = first_matmul(1)
    activation(0, h0)
    activation(1, h0)
    h2 = first_matmul(2)
    activation(2, h1)
    activation(3, h1)
    h3 = first_matmul(3)
    activation(4, h2)
    second_matmul(0, 0)
    activation(5, h2)
    second_matmul(0, 1)
    activation(6, h3)
    second_matmul(0, 2)
    activation(7, h3)
    second_matmul(0, 3)
    for piece in range(acc_scr.shape[0] // PEER_D2):
        second_matmul(1, piece)

    @pl.when(e == pl.num_programs(1) - 1)
    def _():
        y = ALPHA * x_ref[...] + acc_scr[...].T
        out = _layer_norm(y, g_ref[...], beta_ref[...])
        of_ref[...] = out
        ob_ref[...] = out.astype(BF16)


def _peer_experts(x, u, vt, sel, g, b):
    t, d = x.shape
    n_e = u.shape[0]
    resident = dict(pipeline_mode=pl.Buffered(1))
    sel_spec = pl.BlockSpec((PEER_HEADS, N_KEYS, PEER_TT), lambda i, e: (0, 0, i), **resident)
    return pl.pallas_call(
        _peer_expert_kernel,
        grid=(t // PEER_TT, n_e // PEER_TE),
        in_specs=[pl.BlockSpec((PEER_TT, d), lambda i, e: (i, 0), **resident),
                  pl.BlockSpec((PEER_TE, d), lambda i, e: (e, 0)),
                  pl.BlockSpec((d, PEER_TE), lambda i, e: (0, e)),
                  sel_spec, sel_spec, sel_spec, sel_spec,
                  pl.BlockSpec((1, d), lambda i, e: (0, 0)),
                  pl.BlockSpec((1, d), lambda i, e: (0, 0))],
        out_specs=[pl.BlockSpec((PEER_TT, d), lambda i, e: (i, 0)),
                   pl.BlockSpec((PEER_TT, d), lambda i, e: (i, 0))],
        out_shape=[jax.ShapeDtypeStruct((t, d), F32), jax.ShapeDtypeStruct((t, d), BF16)],
        scratch_shapes=[pltpu.VMEM((PEER_TT, d), BF16),
                        pltpu.VMEM((d, PEER_TT), F32),
                        pltpu.VMEM((PEER_TE, PEER_TT), BF16)],
        compiler_params=_params("parallel", "arbitrary"),
        name="peer_experts",
    )(x, u, vt, *sel, g.reshape(1, d), b.reshape(1, d))


def _interleave_groups(w_out):
    ng = D_CONV // LANES
    return w_out.reshape(2, ng, LANES, w_out.shape[-1]).transpose(1, 0, 2, 3).reshape(w_out.shape)


def kernel(x, ev_w_in, ev_conv_a, ev_conv_b_w, ev_conv_b_b, ev_w_r, ev_b_r, ev_w_i, ev_b_i, ev_lam, ev_w_out, od_w_qkv, od_rel_bias, od_w_o, pk_w_q, pk_sub_keys, pk_u, pk_v, ln1_g, ln1_b, ln2_g, ln2_b):
    bsz, seq, d = x.shape
    outs = []
    for bi in range(bsz):
        xf = x[bi]
        xb = xf.astype(BF16)
        for l in range(DEPTH):
            i = l // 2
            if l % 2 == 0:
                z = _matmul(xb, ev_w_in[i].astype(BF16), BF16)
                y = _mixer(z, ev_conv_a[i], ev_conv_b_w[i], ev_conv_b_b[i], ev_w_r[i], ev_b_r[i],
                           ev_w_i[i], ev_b_i[i], ev_lam[i])
                w_o = _interleave_groups(ev_w_out[i]).astype(BF16)
            else:
                qkv = _matmul(xb, od_w_qkv[i].astype(BF16), BF16)
                y = _neighbourhood_attention(qkv, od_rel_bias[i])
                w_o = od_w_o[i].astype(BF16)
            xf, xb = _matmul_res_ln(y, w_o, xf, ln1_g[l], ln1_b[l])
            sel = _peer_select(xb, pk_w_q[l].astype(BF16), pk_sub_keys[l].astype(BF16))
            xf, xb = _peer_experts(xf, pk_u[l].astype(BF16), pk_v[l].T.astype(BF16), sel,
                                   ln2_g[l], ln2_b[l])
        outs.append(xf)
    return jnp.stack(outs)
```

```python
import math

import numpy as np
import jax
import jax.numpy as jnp
from jax import lax
from jax.experimental import pallas as pl
from jax.experimental.pallas import tpu as pltpu

F32 = jnp.float32
BF16 = jnp.bfloat16

D_MODEL = 2048
DEPTH = 4
GRID_W = 64
D_CONV = D_MODEL // 2
D_LRU = D_MODEL // 2
LRU_HEADS = 8
LRU_HD = D_LRU // LRU_HEADS
CONV_A_W = 3
CONV_B_W = 4
RG_C = 8.0
NA_HEADS = 16
NA_HD = D_MODEL // NA_HEADS
WIN_ROWS = 8
WIN_COLS = 16
PEER_HEADS = 8
PEER_QDIM = 256
N_KEYS = 128
PEER_TOPK = 16
ALPHA = (2 * DEPTH) ** 0.25
LN_EPS = 1e-5

LANES = 128
VMEM_LIMIT = 56 * 1024 * 1024
NEG_INF = float("-inf")
MASK_VALUE = -1e30

NT_DIMS = (((1,), (1,)), ((), ()))


def _params(*semantics):
    return pltpu.CompilerParams(dimension_semantics=semantics, vmem_limit_bytes=VMEM_LIMIT)


def _layer_norm(y, g, b):
    mu = jnp.mean(y, axis=-1, keepdims=True)
    yc = y - mu
    var = jnp.mean(yc * yc, axis=-1, keepdims=True)
    return yc * lax.rsqrt(var + LN_EPS) * g + b


def _gelu(x):
    return 0.5 * x * (1.0 + lax.erf(x * (1.0 / math.sqrt(2.0))))


def _matmul_kernel(x_ref, w_ref, o_ref):
    o_ref[...] = jnp.dot(x_ref[...], w_ref[...], preferred_element_type=F32).astype(o_ref.dtype)


def _matmul(x, w, out_dtype, tm=1024, tn=512):
    m, k = x.shape
    n = w.shape[1]
    tm = min(tm, m)
    return pl.pallas_call(
        _matmul_kernel,
        grid=(m // tm, n // tn),
        in_specs=[pl.BlockSpec((tm, k), lambda i, j: (i, 0)),
                  pl.BlockSpec((k, tn), lambda i, j: (0, j))],
        out_specs=pl.BlockSpec((tm, tn), lambda i, j: (i, j)),
        out_shape=jax.ShapeDtypeStruct((m, n), out_dtype),
        compiler_params=_params("parallel", "parallel"),
        name="matmul",
    )(x, w)


def _matmul_res_ln_kernel(a_ref, w_ref, x_ref, g_ref, b_ref, of_ref, ob_ref):
    h = jnp.dot(a_ref[...], w_ref[...], preferred_element_type=F32)
    out = _layer_norm(ALPHA * x_ref[...] + h, g_ref[...], b_ref[...])
    of_ref[...] = out
    ob_ref[...] = out.astype(BF16)


def _matmul_res_ln(a, w, x, g, b, tm=256):
    m, k = a.shape
    n = w.shape[1]
    return pl.pallas_call(
        _matmul_res_ln_kernel,
        grid=(m // tm,),
        in_specs=[pl.BlockSpec((tm, k), lambda i: (i, 0)),
                  pl.BlockSpec((k, n), lambda i: (0, 0)),
                  pl.BlockSpec((tm, n), lambda i: (i, 0)),
                  pl.BlockSpec((1, n), lambda i: (0, 0)),
                  pl.BlockSpec((1, n), lambda i: (0, 0))],
        out_specs=[pl.BlockSpec((tm, n), lambda i: (i, 0)),
                   pl.BlockSpec((tm, n), lambda i: (i, 0))],
        out_shape=[jax.ShapeDtypeStruct((m, n), F32), jax.ShapeDtypeStruct((m, n), BF16)],
        compiler_params=_params("parallel"),
        name="matmul_res_ln",
    )(a, w, x, g.reshape(1, n), b.reshape(1, n))


MIX_TC = 256
MIX_HALO = 16


def _mixer_kernel(gb_ref, gc_ref, xa_ref, gg_ref, xb_ref, ca_ref, cbw_ref, cbb_ref,
                  wr_ref, wi_ref, br_ref, bi_ref, lam_ref, y_ref, hf_scr):
    t_total = xb_ref.shape[0]
    n_chunks = t_total // MIX_TC
    n_ext = MIX_TC + 2 * MIX_HALO
    row = lax.broadcasted_iota(jnp.int32, (MIX_TC, LANES), 0)

    def ext(ref, c):
        t0 = pl.multiple_of(c * MIX_TC, MIX_TC)
        cur = ref[pl.ds(t0, MIX_TC), :].astype(F32)
        lo = pl.multiple_of(jnp.maximum(t0 - MIX_HALO, 0), MIX_HALO)
        hi = pl.multiple_of(jnp.minimum(t0 + MIX_TC, t_total - MIX_HALO), MIX_HALO)
        prev = jnp.where(c > 0, ref[pl.ds(lo, MIX_HALO), :].astype(F32), 0.0)
        nxt = jnp.where(c < n_chunks - 1, ref[pl.ds(hi, MIX_HALO), :].astype(F32), 0.0)
        return jnp.concatenate([prev, cur, nxt], axis=0)

    def shifted(xe, d):
        return pltpu.roll(xe, n_ext - MIX_HALO - d, 0)[:MIX_TC]

    def lru_inputs(c, direction):
        xe = ext(xb_ref, c)
        xc = jnp.zeros((MIX_TC, LANES), F32) + cbb_ref[direction:direction + 1, :]
        for k in range(CONV_B_W):
            d = k if direction == 1 else k - (CONV_B_W - 1)
            xc = xc + cbw_ref[direction, k:k + 1, :] * shifted(xe, d)
        xcb = xc.astype(BF16)
        r = jax.nn.sigmoid(jnp.dot(xcb, wr_ref[direction, 0], preferred_element_type=F32)
                           + br_ref[direction:direction + 1, :])
        i = jax.nn.sigmoid(jnp.dot(xcb, wi_ref[direction, 0], preferred_element_type=F32)
                           + bi_ref[direction:direction + 1, :])
        lam = lam_ref[direction:direction + 1, :]
        softplus_neg_lam = jnp.maximum(-lam, 0.0) + jnp.log1p(jnp.exp(-jnp.abs(lam)))
        log_a = -RG_C * r * softplus_neg_lam
        a = jnp.exp(log_a)
        u = jnp.sqrt(1.0 - jnp.exp(2.0 * log_a)) * i * xc
        return a, u

    def scan_chunk(a, u, carry, reverse):
        s = 1
        while s < MIX_TC:
            if reverse:
                keep = row < MIX_TC - s
                a_sh = jnp.where(keep, pltpu.roll(a, MIX_TC - s, 0), 1.0)
                u_sh = jnp.where(keep, pltpu.roll(u, MIX_TC - s, 0), 0.0)
            else:
                keep = row >= s
                a_sh = jnp.where(keep, pltpu.roll(a, s, 0), 1.0)
                u_sh = jnp.where(keep, pltpu.roll(u, s, 0), 0.0)
            u = a * u_sh + u
            a = a * a_sh
            s *= 2
        return u + a * carry

    def forward_body(c, carry):
        t0 = pl.multiple_of(c * MIX_TC, MIX_TC)
        pe = ext(gc_ref, c) * ext(xa_ref, c)
        conv = jnp.zeros((MIX_TC, LANES), F32)
        for k in range(CONV_A_W):
            conv = conv + ca_ref[k:k + 1, :] * shifted(pe, k - CONV_A_W // 2)
        y_a = gb_ref[pl.ds(t0, MIX_TC), :].astype(F32) * conv
        y_ref[pl.ds(t0, MIX_TC), 0:LANES] = y_a.astype(BF16)
        a, u = lru_inputs(c, 0)
        h = scan_chunk(a, u, carry, reverse=False)
        hf_scr[pl.ds(t0, MIX_TC), :] = h
        return h[MIX_TC - 1:MIX_TC, :]

    def backward_body(j, carry):
        c = n_chunks - 1 - j
        t0 = pl.multiple_of(c * MIX_TC, MIX_TC)
        a, u = lru_inputs(c, 1)
        h = scan_chunk(a, u, carry, reverse=True)
        gate = _gelu(gg_ref[pl.ds(t0, MIX_TC), :].astype(F32))
        y_b = gate * (hf_scr[pl.ds(t0, MIX_TC), :] + h)
        y_ref[pl.ds(t0, MIX_TC), LANES:2 * LANES] = y_b.astype(BF16)
        return h[0:1, :]

    zero = jnp.zeros((1, LANES), F32)
    lax.fori_loop(0, n_chunks, forward_body, zero)
    lax.fori_loop(0, n_chunks, backward_body, zero)


def _mixer(z, conv_a, conv_b_w, conv_b_b, w_r, b_r, w_i, b_i, lam):
    t = z.shape[0]
    ng = D_CONV // LANES

    def zcol(part):
        return pl.BlockSpec((t, LANES), lambda g, part=part: (0, part * ng + g))

    def vec(rows):
        return pl.BlockSpec((rows, LANES), lambda g: (0, g))

    gate_w = pl.BlockSpec((2, 1, LRU_HD, LRU_HD), lambda g: (0, g, 0, 0))
    return pl.pallas_call(
        _mixer_kernel,
        grid=(ng,),
        in_specs=[zcol(0), zcol(1), zcol(2), zcol(3), zcol(4),
                  vec(CONV_A_W),
                  pl.BlockSpec((2, CONV_B_W, LANES), lambda g: (0, 0, g)),
                  vec(2), gate_w, gate_w, vec(2), vec(2), vec(2)],
        out_specs=pl.BlockSpec((t, 2 * LANES), lambda g: (0, g)),
        out_shape=jax.ShapeDtypeStruct((t, D_MODEL), BF16),
        scratch_shapes=[pltpu.VMEM((t, LANES), F32)],
        compiler_params=_params("parallel"),
        name="mixer",
    )(z, z, z, z, z, conv_a, conv_b_w, conv_b_b, w_r.astype(BF16), w_i.astype(BF16), b_r, b_i, lam)


NA_QROWS = 8
NA_KROWS = 16
NA_HPS = 2


def _na_block_plan(rows, r0):
    kh = min(WIN_ROWS, rows)
    kr0 = min(max(r0 - WIN_ROWS // 2, 0), rows - NA_KROWS)
    plan = []
    for qi in range(NA_QROWS):
        qr = r0 + qi
        row_start = min(max(qr - kh // 2, 0), rows - kh)
        for pair in range(NA_KROWS // 2):
            kr = (kr0 + 2 * pair, kr0 + 2 * pair + 1)
            ok = [row_start <= r < row_start + kh for r in kr]
            d = [r - qr + (WIN_ROWS - 1) for r in kr]
            if ok[0] and ok[1]:
                plan.append((qi, pair, 0, d[0]))
            elif ok[0]:
                plan.append((qi, pair, 1, d[0]))
            elif ok[1]:
                plan.append((qi, pair, 2, d[1]))
    return plan


def _na_kernel(q_ref, k_ref, v_ref, tab_ref, o_ref, bias_scr):
    rows = k_ref.shape[0] // GRID_W
    nrb = rows // NA_QROWS
    rb = pl.program_id(1)

    def build(r0):
        bias_scr[...] = jnp.full(bias_scr.shape, MASK_VALUE, F32)
        for hh in range(NA_HPS):
            for qi, pair, kind, d in _na_block_plan(rows, r0):
                bias_scr[hh, qi * GRID_W:(qi + 1) * GRID_W, pair * LANES:(pair + 1) * LANES] = tab_ref[hh, kind, d]

    pl.when(rb == 0)(lambda: build(0))
    if nrb > 2:
        pl.when(rb == 1)(lambda: build(NA_QROWS))
    pl.when(rb == nrb - 1)(lambda: build(rows - NA_QROWS))

    kr0 = jnp.clip(rb * NA_QROWS - WIN_ROWS // 2, 0, rows - NA_KROWS)
    k0 = pl.multiple_of(kr0 * GRID_W, (WIN_ROWS // 2) * GRID_W)
    for hh in range(NA_HPS):
        cols = slice(hh * NA_HD, (hh + 1) * NA_HD)
        k = k_ref[pl.ds(k0, NA_KROWS * GRID_W), cols]
        v = v_ref[pl.ds(k0, NA_KROWS * GRID_W), cols]
        s = lax.dot_general(q_ref[:, cols], k, NT_DIMS, preferred_element_type=F32)
        s = s * (NA_HD ** -0.5) + bias_scr[hh]
        m = jnp.max(s, axis=-1, keepdims=True)
        p = jnp.exp(s - m)
        l = jnp.sum(p, axis=-1, keepdims=True)
        o = jnp.dot(p.astype(BF16), v, preferred_element_type=F32)
        o_ref[:, cols] = (o / l).astype(o_ref.dtype)


def _na_bias_table(rel_bias):
    c = np.arange(GRID_W)
    col_start = np.clip(c - WIN_COLS // 2, 0, GRID_W - WIN_COLS)
    col_ok = (c[None, :] >= col_start[:, None]) & (c[None, :] < col_start[:, None] + WIN_COLS)
    coff = np.clip(c[None, :] - c[:, None], -(WIN_COLS - 1), WIN_COLS - 1) + (WIN_COLS - 1)
    n_coff = 2 * WIN_COLS - 1
    col_onehot = (coff[None] == np.arange(n_coff)[:, None, None]).astype(np.float32)
    by_col = jnp.einsum("hdc,cqk->hdqk", rel_bias, col_onehot, precision=lax.Precision.HIGHEST)
    e = jnp.where(col_ok[None, None], by_col, MASK_VALUE)
    masked = jnp.full_like(e, MASK_VALUE)
    e_next = jnp.concatenate([e[:, 1:], masked[:, :1]], axis=1)
    return jnp.stack([jnp.concatenate([e, e_next], axis=-1),
                      jnp.concatenate([e, masked], axis=-1),
                      jnp.concatenate([masked, e], axis=-1)], axis=1)


def _neighbourhood_attention(qkv, rel_bias):
    t = qkv.shape[0]
    rows = t // GRID_W
    nrb = rows // NA_QROWS
    tq = NA_QROWS * GRID_W
    assert 2 * GRID_W == LANES and rows >= NA_KROWS
    table = _na_bias_table(rel_bias)
    ng = NA_HEADS // NA_HPS
    wide = NA_HPS * NA_HD
    return pl.pallas_call(
        _na_kernel,
        grid=(ng, nrb),
        in_specs=[pl.BlockSpec((tq, wide), lambda g, rb: (rb, g)),
                  pl.BlockSpec((t, wide), lambda g, rb: (0, ng + g)),
                  pl.BlockSpec((t, wide), lambda g, rb: (0, 2 * ng + g)),
                  pl.BlockSpec((NA_HPS,) + table.shape[1:], lambda g, rb: (g, 0, 0, 0, 0))],
        out_specs=pl.BlockSpec((tq, wide), lambda g, rb: (rb, g)),
        out_shape=jax.ShapeDtypeStruct((t, D_MODEL), BF16),
        scratch_shapes=[pltpu.VMEM((NA_HPS, tq, NA_KROWS * GRID_W), F32)],
        compiler_params=_params("parallel", "arbitrary"),
        name="neighbourhood_attention",
    )(qkv, qkv, qkv, table)


SEL_TT = 512
N_TOP = PEER_TOPK + 1
CAND_PAIRS = [(p, q) for p in range(N_TOP) for q in range(N_TOP // (p + 1))]
N_CAND_ROWS = -(-len(CAND_PAIRS) // 8) * 8


def _top_values(arrays, n, ranked):
    rows = [[] for _ in arrays]
    prev = [jnp.full((1, s.shape[1]), jnp.inf, F32) for s in arrays]
    rank = [jnp.zeros_like(s) if ranked[k] else None for k, s in enumerate(arrays)]
    for r in range(n):
        for k, s in enumerate(arrays):
            below = s < prev[k]
            if ranked[k] and r > 0:
                rank[k] = jnp.where(below, float(r), rank[k])
            prev[k] = jnp.max(jnp.where(below, s, NEG_INF), axis=0, keepdims=True)
            rows[k].append(prev[k])
    rank = [jnp.where(s < prev[k], float(n), rank[k]) if ranked[k] else None for k, s in enumerate(arrays)]
    return rows, rank


def _peer_select_kernel(xb_ref, wq_ref, keys_ref, cnt_ref, e1_ref, rank_ref, e2_ref, cand_scr):
    cand_scr[...] = jnp.full(cand_scr.shape, NEG_INF, F32)
    half = PEER_QDIM // 2

    def scores(pair):
        c0 = 2 * pair * PEER_QDIM
        q = jnp.dot(xb_ref[...], wq_ref[:, c0:c0 + 2 * PEER_QDIM], preferred_element_type=F32).astype(BF16)
        return [lax.dot_general(keys_ref[k % 2], q[:, k * half:(k + 1) * half], NT_DIMS,
                                preferred_element_type=F32) for k in range(4)]

    upcoming = scores(0)
    for h in range(PEER_HEADS):
        if h % 2 == 0:
            current = upcoming
            if h + 2 < PEER_HEADS:
                upcoming = scores(h // 2 + 1)
        s1, s2 = current[2 * (h % 2)], current[2 * (h % 2) + 1]
        (a, b), (_, rank) = _top_values([s1, s2], N_TOP, (False, True))
        for n, (p, q_) in enumerate(CAND_PAIRS):
            cand_scr[n:n + 1, :] = a[p] + b[q_]
        cand = cand_scr[...]
        (c,), _ = _top_values([cand], N_TOP, (False,))
        c_last, c_next = c[PEER_TOPK - 1], c[PEER_TOPK]
        thr = 0.5 * (c_last + c_next)
        m = a[0] + b[0]
        z = jnp.sum(jnp.where(cand >= c_last, jnp.exp(cand - m), 0.0), axis=0, keepdims=True)
        need = thr - s1
        cnt = jnp.zeros_like(s1)
        for q_ in range(N_TOP):
            cnt = jnp.where(b[q_] >= need, q_ + 1.0, cnt)
        cnt_ref[h] = cnt
        e1_ref[h] = jnp.exp(s1 - a[0]) / z
        rank_ref[h] = rank.astype(BF16)
        e2_ref[h] = jnp.exp(s2 - b[0]).astype(BF16)


def _peer_select(xb, w_q, sub_keys):
    t, d = xb.shape
    shape = (PEER_HEADS, N_KEYS, t)
    out_spec = pl.BlockSpec((PEER_HEADS, N_KEYS, SEL_TT), lambda i: (0, 0, i))
    return pl.pallas_call(
        _peer_select_kernel,
        grid=(t // SEL_TT,),
        in_specs=[pl.BlockSpec((SEL_TT, d), lambda i: (i, 0)),
                  pl.BlockSpec((d, PEER_HEADS * PEER_QDIM), lambda i: (0, 0)),
                  pl.BlockSpec((2, N_KEYS, PEER_QDIM // 2), lambda i: (0, 0, 0))],
        out_specs=[out_spec] * 4,
        out_shape=[jax.ShapeDtypeStruct(shape, F32), jax.ShapeDtypeStruct(shape, F32),
                   jax.ShapeDtypeStruct(shape, BF16), jax.ShapeDtypeStruct(shape, BF16)],
        scratch_shapes=[pltpu.VMEM((N_CAND_ROWS, SEL_TT), F32)],
        compiler_params=_params("parallel"),
        name="peer_select",
    )(xb, w_q, sub_keys)


PEER_TT = 512
PEER_TE = 1024
PEER_C1 = 256
PEER_C2 = 512
PEER_D2 = 512
SUB16 = 16
assert PEER_TE == 4 * PEER_C1 == 2 * PEER_C2 and D_MODEL == 4 * PEER_D2


def _gated_activation(h_blk, i, cnt_ref, e1_ref, rank_ref, e2_ref, a_scr, row0):
    tt = h_blk.shape[1]
    n_jb = N_KEYS // SUB16
    gates = [None] * n_jb
    for h in range(PEER_HEADS):
        cnt_i = jnp.broadcast_to(cnt_ref[h, pl.ds(i, 1), :], (SUB16, tt)).astype(BF16)
        e1_i = jnp.broadcast_to(e1_ref[h, pl.ds(i, 1), :], (SUB16, tt)).astype(BF16)
        for jb in range(n_jb):
            rows = slice(jb * SUB16, (jb + 1) * SUB16)
            term = jnp.where(rank_ref[h, rows, :] < cnt_i, e2_ref[h, rows, :], 0) * e1_i
            gates[jb] = term if h == 0 else gates[jb] + term
    for jb in range(n_jb):
        rows = slice(jb * SUB16, (jb + 1) * SUB16)
        act = _gelu(h_blk[rows])
        a_scr[row0 + jb * SUB16:row0 + (jb + 1) * SUB16, :] = act.astype(BF16) * gates[jb]


def _peer_expert_kernel(x_ref, u_ref, vt_ref, cnt_ref, e1_ref, rank_ref, e2_ref, g_ref, beta_ref,
                        of_ref, ob_ref, xb_scr, acc_scr, a_scr):
    e = pl.program_id(1)
    n_i = PEER_TE // N_KEYS

    @pl.when(e == 0)
    def _():
        xb_scr[...] = x_ref[...].astype(BF16)
        acc_scr[...] = jnp.zeros(acc_scr.shape, F32)

    xb = xb_scr[...]

    def first_matmul(c):
        r0 = c * PEER_C1
        return lax.dot_general(u_ref[r0:r0 + PEER_C1, :], xb, NT_DIMS, preferred_element_type=F32)

    def activation(ii, h_t):
        r = (ii * N_KEYS) % PEER_C1
        _gated_activation(h_t[r:r + N_KEYS], e * n_i + ii, cnt_ref, e1_ref, rank_ref, e2_ref, a_scr, ii * N_KEYS)

    def second_matmul(c2, piece):
        r0, k0 = piece * PEER_D2, c2 * PEER_C2
        acc_scr[r0:r0 + PEER_D2, :] += jnp.dot(vt_ref[r0:r0 + PEER_D2, k0:k0 + PEER_C2], a_scr[k0:k0 + PEER_C2, :],
                                               preferred_element_type=F32)

    h0 = first_matmul(0)
    h1 = first_matmul(1)
    activation(0, h0)
    activation(1, h0)
    h2 = first_matmul(2)
    activation(2, h1)
    activation(3, h1)
    h3 = first_matmul(3)
    activation(4, h2)
    second_matmul(0, 0)
    activation(5, h2)
    second_matmul(0, 1)
    activation(6, h3)
    second_matmul(0, 2)
    activation(7, h3)
    second_matmul(0, 3)
    for piece in range(acc_scr.shape[0] // PEER_D2):
        second_matmul(1, piece)

    @pl.when(e == pl.num_programs(1) - 1)
    def _():
        y = ALPHA * x_ref[...] + acc_scr[...].T
        out = _layer_norm(y, g_ref[...], beta_ref[...])
        of_ref[...] = out
        ob_ref[...] = out.astype(BF16)


def _peer_experts(x, u, vt, sel, g, b):
    t, d = x.shape
    n_e = u.shape[0]
    resident = dict(pipeline_mode=pl.Buffered(1))
    sel_spec = pl.BlockSpec((PEER_HEADS, N_KEYS, PEER_TT), lambda i, e: (0, 0, i), **resident)
    return pl.pallas_call(
        _peer_expert_kernel,
        grid=(t // PEER_TT, n_e // PEER_TE),
        in_specs=[pl.BlockSpec((PEER_TT, d), lambda i, e: (i, 0), **resident),
                  pl.BlockSpec((PEER_TE, d), lambda i, e: (e, 0)),
                  pl.BlockSpec((d, PEER_TE), lambda i, e: (0, e)),
                  sel_spec, sel_spec, sel_spec, sel_spec,
                  pl.BlockSpec((1, d), lambda i, e: (0, 0)),
                  pl.BlockSpec((1, d), lambda i, e: (0, 0))],
        out_specs=[pl.BlockSpec((PEER_TT, d), lambda i, e: (i, 0)),
                   pl.BlockSpec((PEER_TT, d), lambda i, e: (i, 0))],
        out_shape=[jax.ShapeDtypeStruct((t, d), F32), jax.ShapeDtypeStruct((t, d), BF16)],
        scratch_shapes=[pltpu.VMEM((PEER_TT, d), BF16),
                        pltpu.VMEM((d, PEER_TT), F32),
                        pltpu.VMEM((PEER_TE, PEER_TT), BF16)],
        compiler_params=_params("parallel", "arbitrary"),
        name="peer_experts",
    )(x, u, vt, *sel, g.reshape(1, d), b.reshape(1, d))


def _interleave_groups(w_out):
    ng = D_CONV // LANES
    return w_out.reshape(2, ng, LANES, w_out.shape[-1]).transpose(1, 0, 2, 3).reshape(w_out.shape)


def kernel(x, ev_w_in, ev_conv_a, ev_conv_b_w, ev_conv_b_b, ev_w_r, ev_b_r, ev_w_i, ev_b_i, ev_lam, ev_w_out, od_w_qkv, od_rel_bias, od_w_o, pk_w_q, pk_sub_keys, pk_u, pk_v, ln1_g, ln1_b, ln2_g, ln2_b):
    bsz, seq, d = x.shape
    outs = []
    for bi in range(bsz):
        xf = x[bi]
        xb = xf.astype(BF16)
        for l in range(DEPTH):
            i = l // 2
            if l % 2 == 0:
                z = _matmul(xb, ev_w_in[i].astype(BF16), BF16)
                y = _mixer(z, ev_conv_a[i], ev_conv_b_w[i], ev_conv_b_b[i], ev_w_r[i], ev_b_r[i],
                           ev_w_i[i], ev_b_i[i], ev_lam[i])
                w_o = _interleave_groups(ev_w_out[i]).astype(BF16)
            else:
                qkv = _matmul(xb, od_w_qkv[i].astype(BF16), BF16)
                y = _neighbourhood_attention(qkv, od_rel_bias[i])
                w_o = od_w_o[i].astype(BF16)
            xf, xb = _matmul_res_ln(y, w_o, xf, ln1_g[l], ln1_b[l])
            sel = _peer_select(xb, pk_w_q[l].astype(BF16), pk_sub_keys[l].astype(BF16))
            xf, xb = _peer_experts(xf, pk_u[l].astype(BF16), pk_v[l].T.astype(BF16), sel,
                                   ln2_g[l], ln2_b[l])
        outs.append(xf)
    return jnp.stack(outs)
```

```python
import math

import numpy as np
import jax
import jax.numpy as jnp
from jax import lax
from jax.experimental import pallas as pl
from jax.experimental.pallas import tpu as pltpu

F32 = jnp.float32
BF16 = jnp.bfloat16

D_MODEL = 2048
DEPTH = 4
GRID_W = 64
D_CONV = D_MODEL // 2
D_LRU = D_MODEL // 2
LRU_HEADS = 8
LRU_HD = D_LRU // LRU_HEADS
CONV_A_W = 3
CONV_B_W = 4
RG_C = 8.0
NA_HEADS = 16
NA_HD = D_MODEL // NA_HEADS
WIN_ROWS = 8
WIN_COLS = 16
PEER_HEADS = 8
PEER_QDIM = 256
N_KEYS = 128
PEER_TOPK = 16
ALPHA = (2 * DEPTH) ** 0.25
LN_EPS = 1e-5

LANES = 128
VMEM_LIMIT = 56 * 1024 * 1024
NEG_INF = float("-inf")
MASK_VALUE = -1e30

NT_DIMS = (((1,), (1,)), ((), ()))


def _params(*semantics):
    return pltpu.CompilerParams(dimension_semantics=semantics, vmem_limit_bytes=VMEM_LIMIT)


def _layer_norm(y, g, b):
    mu = jnp.mean(y, axis=-1, keepdims=True)
    yc = y - mu
    var = jnp.mean(yc * yc, axis=-1, keepdims=True)
    return yc * lax.rsqrt(var + LN_EPS) * g + b


def _gelu(x):
    return 0.5 * x * (1.0 + lax.erf(x * (1.0 / math.sqrt(2.0))))


def _matmul_kernel(x_ref, w_ref, o_ref):
    o_ref[...] = jnp.dot(x_ref[...], w_ref[...], preferred_element_type=F32).astype(o_ref.dtype)


def _matmul(x, w_stack, layer, out_dtype, tm=1024, tn=512):
    m, k = x.shape
    n = w_stack.shape[2]
    tm = min(tm, m)
    return pl.pallas_call(
        _matmul_kernel,
        grid=(m // tm, n // tn),
        in_specs=[pl.BlockSpec((tm, k), lambda i, j: (i, 0)),
                  pl.BlockSpec((None, k, tn), lambda i, j: (layer, 0, j))],
        out_specs=pl.BlockSpec((tm, tn), lambda i, j: (i, j)),
        out_shape=jax.ShapeDtypeStruct((m, n), out_dtype),
        compiler_params=_params("parallel", "parallel"),
        name="matmul",
    )(x, w_stack)


def _matmul_res_ln_kernel(a_ref, w_ref, x_ref, g_ref, b_ref, of_ref, ob_ref):
    h = jnp.dot(a_ref[...], w_ref[...], preferred_element_type=F32)
    out = _layer_norm(ALPHA * x_ref[...] + h, g_ref[...], b_ref[...])
    of_ref[...] = out
    ob_ref[...] = out.astype(BF16)


def _matmul_res_ln(a, w_stack, layer, x, g, b, tm=256):
    m, k = a.shape
    n = w_stack.shape[2]
    return pl.pallas_call(
        _matmul_res_ln_kernel,
        grid=(m // tm,),
        in_specs=[pl.BlockSpec((tm, k), lambda i: (i, 0)),
                  pl.BlockSpec((None, k, n), lambda i: (layer, 0, 0)),
                  pl.BlockSpec((tm, n), lambda i: (i, 0)),
                  pl.BlockSpec((1, n), lambda i: (0, 0)),
                  pl.BlockSpec((1, n), lambda i: (0, 0))],
        out_specs=[pl.BlockSpec((tm, n), lambda i: (i, 0)),
                   pl.BlockSpec((tm, n), lambda i: (i, 0))],
        out_shape=[jax.ShapeDtypeStruct((m, n), F32), jax.ShapeDtypeStruct((m, n), BF16)],
        compiler_params=_params("parallel"),
        name="matmul_res_ln",
    )(a, w_stack, x, g.reshape(1, n), b.reshape(1, n))


MIX_TC = 256
MIX_HALO = 16


def _mixer_kernel(gb_ref, gc_ref, xa_ref, gg_ref, xb_ref, ca_ref, cbw_ref, cbb_ref,
                  wr_ref, wi_ref, br_ref, bi_ref, lam_ref, y_ref, hf_scr):
    t_total = xb_ref.shape[0]
    n_chunks = t_total // MIX_TC
    n_ext = MIX_TC + 2 * MIX_HALO
    row = lax.broadcasted_iota(jnp.int32, (MIX_TC, LANES), 0)

    def ext(ref, c):
        t0 = pl.multiple_of(c * MIX_TC, MIX_TC)
        cur = ref[pl.ds(t0, MIX_TC), :].astype(F32)
        lo = pl.multiple_of(jnp.maximum(t0 - MIX_HALO, 0), MIX_HALO)
        hi = pl.multiple_of(jnp.minimum(t0 + MIX_TC, t_total - MIX_HALO), MIX_HALO)
        prev = jnp.where(c > 0, ref[pl.ds(lo, MIX_HALO), :].astype(F32), 0.0)
        nxt = jnp.where(c < n_chunks - 1, ref[pl.ds(hi, MIX_HALO), :].astype(F32), 0.0)
        return jnp.concatenate([prev, cur, nxt], axis=0)

    def shifted(xe, d):
        return pltpu.roll(xe, n_ext - MIX_HALO - d, 0)[:MIX_TC]

    def lru_inputs(c, direction):
        xe = ext(xb_ref, c)
        xc = jnp.zeros((MIX_TC, LANES), F32) + cbb_ref[direction:direction + 1, :]
        for k in range(CONV_B_W):
            d = k if direction == 1 else k - (CONV_B_W - 1)
            xc = xc + cbw_ref[direction, k:k + 1, :] * shifted(xe, d)
        xcb = xc.astype(BF16)
        r = jax.nn.sigmoid(jnp.dot(xcb, wr_ref[direction, 0], preferred_element_type=F32)
                           + br_ref[direction:direction + 1, :])
        i = jax.nn.sigmoid(jnp.dot(xcb, wi_ref[direction, 0], preferred_element_type=F32)
                           + bi_ref[direction:direction + 1, :])
        lam = lam_ref[direction:direction + 1, :]
        softplus_neg_lam = jnp.maximum(-lam, 0.0) + jnp.log1p(jnp.exp(-jnp.abs(lam)))
        log_a = -RG_C * r * softplus_neg_lam
        a = jnp.exp(log_a)
        u = jnp.sqrt(1.0 - jnp.exp(2.0 * log_a)) * i * xc
        return a, u

    def scan_chunk(a, u, carry, reverse):
        s = 1
        while s < MIX_TC:
            if reverse:
                keep = row < MIX_TC - s
                a_sh = jnp.where(keep, pltpu.roll(a, MIX_TC - s, 0), 1.0)
                u_sh = jnp.where(keep, pltpu.roll(u, MIX_TC - s, 0), 0.0)
            else:
                keep = row >= s
                a_sh = jnp.where(keep, pltpu.roll(a, s, 0), 1.0)
                u_sh = jnp.where(keep, pltpu.roll(u, s, 0), 0.0)
            u = a * u_sh + u
            a = a * a_sh
            s *= 2
        return u + a * carry

    def forward_body(c, carry):
        t0 = pl.multiple_of(c * MIX_TC, MIX_TC)
        pe = ext(gc_ref, c) * ext(xa_ref, c)
        conv = jnp.zeros((MIX_TC, LANES), F32)
        for k in range(CONV_A_W):
            conv = conv + ca_ref[k:k + 1, :] * shifted(pe, k - CONV_A_W // 2)
        y_a = gb_ref[pl.ds(t0, MIX_TC), :].astype(F32) * conv
        y_ref[pl.ds(t0, MIX_TC), 0:LANES] = y_a.astype(BF16)
        a, u = lru_inputs(c, 0)
        h = scan_chunk(a, u, carry, reverse=False)
        hf_scr[pl.ds(t0, MIX_TC), :] = h
        return h[MIX_TC - 1:MIX_TC, :]

    def backward_body(j, carry):
        c = n_chunks - 1 - j
        t0 = pl.multiple_of(c * MIX_TC, MIX_TC)
        a, u = lru_inputs(c, 1)
        h = scan_chunk(a, u, carry, reverse=True)
        gate = _gelu(gg_ref[pl.ds(t0, MIX_TC), :].astype(F32))
        y_b = gate * (hf_scr[pl.ds(t0, MIX_TC), :] + h)
        y_ref[pl.ds(t0, MIX_TC), LANES:2 * LANES] = y_b.astype(BF16)
        return h[0:1, :]

    zero = jnp.zeros((1, LANES), F32)
    lax.fori_loop(0, n_chunks, forward_body, zero)
    lax.fori_loop(0, n_chunks, backward_body, zero)


def _mixer(z, conv_a, conv_b_w, conv_b_b, w_r, b_r, w_i, b_i, lam):
    t = z.shape[0]
    ng = D_CONV // LANES

    def zcol(part):
        return pl.BlockSpec((t, LANES), lambda g, part=part: (0, part * ng + g))

    def vec(rows):
        return pl.BlockSpec((rows, LANES), lambda g: (0, g))

    gate_w = pl.BlockSpec((2, 1, LRU_HD, LRU_HD), lambda g: (0, g, 0, 0))
    return pl.pallas_call(
        _mixer_kernel,
        grid=(ng,),
        in_specs=[zcol(0), zcol(1), zcol(2), zcol(3), zcol(4),
                  vec(CONV_A_W),
                  pl.BlockSpec((2, CONV_B_W, LANES), lambda g: (0, 0, g)),
                  vec(2), gate_w, gate_w, vec(2), vec(2), vec(2)],
        out_specs=pl.BlockSpec((t, 2 * LANES), lambda g: (0, g)),
        out_shape=jax.ShapeDtypeStruct((t, D_MODEL), BF16),
        scratch_shapes=[pltpu.VMEM((t, LANES), F32)],
        compiler_params=_params("parallel"),
        name="mixer",
    )(z, z, z, z, z, conv_a, conv_b_w, conv_b_b, w_r.astype(BF16), w_i.astype(BF16), b_r, b_i, lam)


NA_QROWS = 8
NA_KROWS = 16
NA_HPS = 4


def _na_block_plan(rows, r0):
    kh = min(WIN_ROWS, rows)
    kr0 = min(max(r0 - WIN_ROWS // 2, 0), rows - NA_KROWS)
    plan = []
    for qi in range(NA_QROWS):
        qr = r0 + qi
        row_start = min(max(qr - kh // 2, 0), rows - kh)
        for pair in range(NA_KROWS // 2):
            kr = (kr0 + 2 * pair, kr0 + 2 * pair + 1)
            ok = [row_start <= r < row_start + kh for r in kr]
            d = [r - qr + (WIN_ROWS - 1) for r in kr]
            if ok[0] and ok[1]:
                plan.append((qi, pair, 0, d[0]))
            elif ok[0]:
                plan.append((qi, pair, 1, d[0]))
            elif ok[1]:
                plan.append((qi, pair, 2, d[1]))
    return plan


def _na_kernel(q_ref, k_ref, v_ref, tab_ref, o_ref, bias_scr):
    rows = k_ref.shape[0] // GRID_W
    nrb = rows // NA_QROWS
    rb = pl.program_id(1)

    def build(r0):
        bias_scr[...] = jnp.full(bias_scr.shape, MASK_VALUE, F32)
        for hh in range(NA_HPS):
            for qi, pair, kind, d in _na_block_plan(rows, r0):
                bias_scr[hh, qi * GRID_W:(qi + 1) * GRID_W, pair * LANES:(pair + 1) * LANES] = tab_ref[hh, kind, d]

    pl.when(rb == 0)(lambda: build(0))
    if nrb > 2:
        pl.when(rb == 1)(lambda: build(NA_QROWS))
    pl.when(rb == nrb - 1)(lambda: build(rows - NA_QROWS))

    kr0 = jnp.clip(rb * NA_QROWS - WIN_ROWS // 2, 0, rows - NA_KROWS)
    k0 = pl.multiple_of(kr0 * GRID_W, (WIN_ROWS // 2) * GRID_W)
    for hh in range(NA_HPS):
        cols = slice(hh * NA_HD, (hh + 1) * NA_HD)
        k = k_ref[pl.ds(k0, NA_KROWS * GRID_W), cols]
        v = v_ref[pl.ds(k0, NA_KROWS * GRID_W), cols]
        s = lax.dot_general(q_ref[:, cols], k, NT_DIMS, preferred_element_type=F32)
        s = s * (NA_HD ** -0.5) + bias_scr[hh]
        m = jnp.max(s, axis=-1, keepdims=True)
        p = jnp.exp(s - m)
        l = jnp.sum(p, axis=-1, keepdims=True)
        o = jnp.dot(p.astype(BF16), v, preferred_element_type=F32)
        o_ref[:, cols] = (o / l).astype(o_ref.dtype)


def _na_bias_table(rel_bias):
    c = np.arange(GRID_W)
    col_start = np.clip(c - WIN_COLS // 2, 0, GRID_W - WIN_COLS)
    col_ok = (c[None, :] >= col_start[:, None]) & (c[None, :] < col_start[:, None] + WIN_COLS)
    coff = np.clip(c[None, :] - c[:, None], -(WIN_COLS - 1), WIN_COLS - 1) + (WIN_COLS - 1)
    n_coff = 2 * WIN_COLS - 1
    col_onehot = (coff[None] == np.arange(n_coff)[:, None, None]).astype(np.float32)
    by_col = jnp.einsum("hdc,cqk->hdqk", rel_bias, col_onehot, precision=lax.Precision.HIGHEST)
    e = jnp.where(col_ok[None, None], by_col, MASK_VALUE)
    masked = jnp.full_like(e, MASK_VALUE)
    e_next = jnp.concatenate([e[:, 1:], masked[:, :1]], axis=1)
    return jnp.stack([jnp.concatenate([e, e_next], axis=-1),
                      jnp.concatenate([e, masked], axis=-1),
                      jnp.concatenate([masked, e], axis=-1)], axis=1)


def _neighbourhood_attention(qkv, rel_bias):
    t = qkv.shape[0]
    rows = t // GRID_W
    nrb = rows // NA_QROWS
    tq = NA_QROWS * GRID_W
    assert 2 * GRID_W == LANES and rows >= NA_KROWS
    table = _na_bias_table(rel_bias)
    ng = NA_HEADS // NA_HPS
    wide = NA_HPS * NA_HD
    resident = dict(pipeline_mode=pl.Buffered(1))
    return pl.pallas_call(
        _na_kernel,
        grid=(ng, nrb),
        in_specs=[pl.BlockSpec((tq, wide), lambda g, rb: (rb, g)),
                  pl.BlockSpec((t, wide), lambda g, rb: (0, ng + g), **resident),
                  pl.BlockSpec((t, wide), lambda g, rb: (0, 2 * ng + g), **resident),
                  pl.BlockSpec((NA_HPS,) + table.shape[1:], lambda g, rb: (g, 0, 0, 0, 0), **resident)],
        out_specs=pl.BlockSpec((tq, wide), lambda g, rb: (rb, g)),
        out_shape=jax.ShapeDtypeStruct((t, D_MODEL), BF16),
        scratch_shapes=[pltpu.VMEM((NA_HPS, tq, NA_KROWS * GRID_W), F32)],
        compiler_params=_params("parallel", "arbitrary"),
        name="neighbourhood_attention",
    )(qkv, qkv, qkv, table)


SEL_TT = 512
N_TOP = PEER_TOPK + 1
CAND_PAIRS = [(p, q) for p in range(N_TOP) for q in range(N_TOP // (p + 1))]
N_CAND_ROWS = -(-len(CAND_PAIRS) // 8) * 8


def _top_values(arrays, n, ranked, fillers=None):
    rows = [[] for _ in arrays]
    prev = [jnp.full((1, s.shape[1]), jnp.inf, F32) for s in arrays]
    rank = [jnp.zeros_like(s) if ranked[k] else None for k, s in enumerate(arrays)]
    for r in range(n):
        for k, s in enumerate(arrays):
            below = s < prev[k]
            if ranked[k] and r > 0:
                rank[k] = jnp.where(below, float(r), rank[k])
            prev[k] = jnp.max(jnp.where(below, s, NEG_INF), axis=0, keepdims=True)
            rows[k].append(prev[k])
        if fillers and r in fillers:
            fillers[r]()
    rank = [jnp.where(s < prev[k], float(n), rank[k]) if ranked[k] else None for k, s in enumerate(arrays)]
    return rows, rank


SEL_CHUNK = 128
SEL_FILL_PASSES = (4, 12)


def _peer_select_kernel(xb_ref, wq_ref, keys_ref, cnt_ref, e1_ref, rank_ref, e2_ref, cand_scr):
    cand_scr[...] = jnp.full(cand_scr.shape, NEG_INF, F32)
    half = PEER_QDIM // 2
    n_chunks = xb_ref.shape[0] // SEL_CHUNK
    n_pairs = PEER_HEADS // 2
    pieces = [[] for _ in range(n_pairs)]

    def score_chunk(pair, c):
        c0 = 2 * pair * PEER_QDIM
        q = jnp.dot(xb_ref[c * SEL_CHUNK:(c + 1) * SEL_CHUNK, :], wq_ref[:, c0:c0 + 2 * PEER_QDIM],
                    preferred_element_type=F32).astype(BF16)
        pieces[pair].append([lax.dot_general(keys_ref[k % 2], q[:, k * half:(k + 1) * half], NT_DIMS,
                                             preferred_element_type=F32) for k in range(4)])

    def pair_scores(pair, k):
        return jnp.concatenate([p[k] for p in pieces[pair]], axis=1)

    for c in range(n_chunks):
        score_chunk(0, c)
    per_head = n_chunks // 2
    assert per_head == len(SEL_FILL_PASSES)
    for h in range(PEER_HEADS):
        pair = h // 2
        s1, s2 = pair_scores(pair, 2 * (h % 2)), pair_scores(pair, 2 * (h % 2) + 1)
        fillers = None
        if pair + 1 < n_pairs:
            fillers = {r: (lambda c=(h % 2) * per_head + n: score_chunk(pair + 1, c))
                       for n, r in enumerate(SEL_FILL_PASSES)}
        (a, b), (_, rank) = _top_values([s1, s2], N_TOP, (False, True), fillers)
        for n, (p, q_) in enumerate(CAND_PAIRS):
            cand_scr[n:n + 1, :] = a[p] + b[q_]
        cand = cand_scr[...]
        (c,), _ = _top_values([cand], N_TOP, (False,))
        c_last, c_next = c[PEER_TOPK - 1], c[PEER_TOPK]
        thr = 0.5 * (c_last + c_next)
        m = a[0] + b[0]
        z = jnp.sum(jnp.where(cand >= c_last, jnp.exp(cand - m), 0.0), axis=0, keepdims=True)
        need = thr - s1
        cnt = jnp.zeros_like(s1)
        for q_ in range(N_TOP):
            cnt = jnp.where(b[q_] >= need, q_ + 1.0, cnt)
        cnt_ref[h] = cnt
        e1_ref[h] = jnp.exp(s1 - a[0]) / z
        rank_ref[h] = rank.astype(BF16)
        e2_ref[h] = jnp.exp(s2 - b[0]).astype(BF16)


def _peer_select(xb, w_q_stack, sub_keys_stack, layer):
    t, d = xb.shape
    shape = (PEER_HEADS, N_KEYS, t)
    out_spec = pl.BlockSpec((PEER_HEADS, N_KEYS, SEL_TT), lambda i: (0, 0, i))
    return pl.pallas_call(
        _peer_select_kernel,
        grid=(t // SEL_TT,),
        in_specs=[pl.BlockSpec((SEL_TT, d), lambda i: (i, 0)),
                  pl.BlockSpec((None, d, PEER_HEADS * PEER_QDIM), lambda i: (layer, 0, 0)),
                  pl.BlockSpec((None, 2, N_KEYS, PEER_QDIM // 2), lambda i: (layer, 0, 0, 0))],
        out_specs=[out_spec] * 4,
        out_shape=[jax.ShapeDtypeStruct(shape, F32), jax.ShapeDtypeStruct(shape, F32),
                   jax.ShapeDtypeStruct(shape, BF16), jax.ShapeDtypeStruct(shape, BF16)],
        scratch_shapes=[pltpu.VMEM((N_CAND_ROWS, SEL_TT), F32)],
        compiler_params=_params("parallel"),
        name="peer_select",
    )(xb, w_q_stack, sub_keys_stack)


PEER_TT = 512
PEER_TE = 1024
PEER_C1 = 256
PEER_C2 = 512
PEER_D2 = 512
SUB16 = 16
assert PEER_TE == 4 * PEER_C1 == 2 * PEER_C2 and D_MODEL == 4 * PEER_D2


def _gated_activation(h_blk, i, cnt_ref, e1_ref, rank_ref, e2_ref, a_scr, row0):
    tt = h_blk.shape[1]
    n_jb = N_KEYS // SUB16
    gates = [None] * n_jb
    for h in range(PEER_HEADS):
        cnt_i = jnp.broadcast_to(cnt_ref[h, pl.ds(i, 1), :], (SUB16, tt)).astype(BF16)
        e1_i = jnp.broadcast_to(e1_ref[h, pl.ds(i, 1), :], (SUB16, tt)).astype(BF16)
        for jb in range(n_jb):
            rows = slice(jb * SUB16, (jb + 1) * SUB16)
            term = jnp.where(rank_ref[h, rows, :] < cnt_i, e2_ref[h, rows, :], 0) * e1_i
            gates[jb] = term if h == 0 else gates[jb] + term
    for jb in range(n_jb):
        rows = slice(jb * SUB16, (jb + 1) * SUB16)
        act = _gelu(h_blk[rows])
        a_scr[row0 + jb * SUB16:row0 + (jb + 1) * SUB16, :] = act.astype(BF16) * gates[jb]


def _peer_expert_kernel(x_ref, u_ref, vt_ref, cnt_ref, e1_ref, rank_ref, e2_ref, g_ref, beta_ref,
                        of_ref, ob_ref, xb_scr, acc_scr, a_scr):
    e = pl.program_id(1)
    n_i = PEER_TE // N_KEYS

    @pl.when(e == 0)
    def _():
        xb_scr[...] = x_ref[...].astype(BF16)
        acc_scr[...] = jnp.zeros(acc_scr.shape, F32)

    xb = xb_scr[...]

    def first_matmul(c):
        r0 = c * PEER_C1
        return lax.dot_general(u_ref[r0:r0 + PEER_C1, :], xb, NT_DIMS, preferred_element_type=F32)

    def activation(ii, h_t):
        r = (ii * N_KEYS) % PEER_C1
        _gated_activation(h_t[r:r + N_KEYS], e * n_i + ii, cnt_ref, e1_ref, rank_ref, e2_ref, a_scr, ii * N_KEYS)

    def second_matmul(c2, piece):
        r0, k0 = piece * PEER_D2, c2 * PEER_C2
        acc_scr[r0:r0 + PEER_D2, :] += jnp.dot(vt_ref[r0:r0 + PEER_D2, k0:k0 + PEER_C2], a_scr[k0:k0 + PEER_C2, :],
                                               preferred_element_type=F32)

    h0 = first_matmul(0)
    h1 = first_matmul(1)
    activation(0, h0)
    activation(1, h0)
    h2 = first_matmul(2)
    activation(2, h1)
    activation(3, h1)
    h3 = first_matmul(3)
    activation(4, h2)
    second_matmul(0, 0)
    activation(5, h2)
    second_matmul(0, 1)
    activation(6, h3)
    second_matmul(0, 2)
    activation(7, h3)
    second_matmul(0, 3)
    for piece in range(acc_scr.shape[0] // PEER_D2):
        second_matmul(1, piece)

    @pl.when(e == pl.num_programs(1) - 1)
    def _():
        y = ALPHA * x_ref[...] + acc_scr[...].T
        out = _layer_norm(y, g_ref[...], beta_ref[...])
        of_ref[...] = out
        ob_ref[...] = out.astype(BF16)


def _peer_experts(x, u_stack, vt_stack, layer, sel, g, b):
    t, d = x.shape
    n_e = u_stack.shape[1]
    resident = dict(pipeline_mode=pl.Buffered(1))
    sel_spec = pl.BlockSpec((PEER_HEADS, N_KEYS, PEER_TT), lambda i, e: (0, 0, i), **resident)
    return pl.pallas_call(
        _peer_expert_kernel,
        grid=(t // PEER_TT, n_e // PEER_TE),
        in_specs=[pl.BlockSpec((PEER_TT, d), lambda i, e: (i, 0), **resident),
                  pl.BlockSpec((None, PEER_TE, d), lambda i, e: (layer, e, 0)),
                  pl.BlockSpec((None, d, PEER_TE), lambda i, e: (layer, 0, e)),
                  sel_spec, sel_spec, sel_spec, sel_spec,
                  pl.BlockSpec((1, d), lambda i, e: (0, 0)),
                  pl.BlockSpec((1, d), lambda i, e: (0, 0))],
        out_specs=[pl.BlockSpec((PEER_TT, d), lambda i, e: (i, 0)),
                   pl.BlockSpec((PEER_TT, d), lambda i, e: (i, 0))],
        out_shape=[jax.ShapeDtypeStruct((t, d), F32), jax.ShapeDtypeStruct((t, d), BF16)],
        scratch_shapes=[pltpu.VMEM((PEER_TT, d), BF16),
                        pltpu.VMEM((d, PEER_TT), F32),
                        pltpu.VMEM((PEER_TE, PEER_TT), BF16)],
        compiler_params=_params("parallel", "arbitrary"),
        name="peer_experts",
    )(x, u_stack, vt_stack, *sel, g.reshape(1, d), b.reshape(1, d))


def _interleave_groups(w_out):
    ng = D_CONV // LANES
    n = w_out.shape[0]
    return w_out.reshape(n, 2, ng, LANES, w_out.shape[-1]).transpose(0, 2, 1, 3, 4).reshape(w_out.shape)


def kernel(x, ev_w_in, ev_conv_a, ev_conv_b_w, ev_conv_b_b, ev_w_r, ev_b_r, ev_w_i, ev_b_i, ev_lam, ev_w_out, od_w_qkv, od_rel_bias, od_w_o, pk_w_q, pk_sub_keys, pk_u, pk_v, ln1_g, ln1_b, ln2_g, ln2_b):
    bsz, seq, d = x.shape
    w_in = ev_w_in.astype(BF16)
    w_out = _interleave_groups(ev_w_out).astype(BF16)
    w_qkv = od_w_qkv.astype(BF16)
    w_o = od_w_o.astype(BF16)
    w_q = pk_w_q.astype(BF16)
    sub_keys = pk_sub_keys.astype(BF16)
    u = pk_u.astype(BF16)
    vt = jnp.swapaxes(pk_v, 1, 2).astype(BF16)
    outs = []
    for bi in range(bsz):
        xf = x[bi]
        xb = xf.astype(BF16)
        for l in range(DEPTH):
            i = l // 2
            if l % 2 == 0:
                z = _matmul(xb, w_in, i, BF16)
                y = _mixer(z, ev_conv_a[i], ev_conv_b_w[i], ev_conv_b_b[i], ev_w_r[i], ev_b_r[i],
                           ev_w_i[i], ev_b_i[i], ev_lam[i])
                w_mix = w_out
            else:
                qkv = _matmul(xb, w_qkv, i, BF16)
                y = _neighbourhood_attention(qkv, od_rel_bias[i])
                w_mix = w_o
            xf, xb = _matmul_res_ln(y, w_mix, i, xf, ln1_g[l], ln1_b[l])
            sel = _peer_select(xb, w_q, sub_keys, l)
            xf, xb = _peer_experts(xf, u, vt, l, sel, ln2_g[l], ln2_b[l])
        outs.append(xf)
    return jnp.stack(outs)
```

```python
import math

import numpy as np
import jax
import jax.numpy as jnp
from jax import lax
from jax.experimental import pallas as pl
from jax.experimental.pallas import tpu as pltpu

F32 = jnp.float32
BF16 = jnp.bfloat16

D_MODEL = 2048
DEPTH = 4
GRID_W = 64
D_CONV = D_MODEL // 2
D_LRU = D_MODEL // 2
LRU_HEADS = 8
LRU_HD = D_LRU // LRU_HEADS
CONV_A_W = 3
CONV_B_W = 4
RG_C = 8.0
NA_HEADS = 16
NA_HD = D_MODEL // NA_HEADS
WIN_ROWS = 8
WIN_COLS = 16
PEER_HEADS = 8
PEER_QDIM = 256
N_KEYS = 128
PEER_TOPK = 16
ALPHA = (2 * DEPTH) ** 0.25
LN_EPS = 1e-5

LANES = 128
VMEM_LIMIT = 56 * 1024 * 1024
NEG_INF = float("-inf")
MASK_VALUE = -1e30

NT_DIMS = (((1,), (1,)), ((), ()))


def _params(*semantics):
    return pltpu.CompilerParams(dimension_semantics=semantics, vmem_limit_bytes=VMEM_LIMIT)


def _layer_norm(y, g, b):
    mu = jnp.mean(y, axis=-1, keepdims=True)
    yc = y - mu
    var = jnp.mean(yc * yc, axis=-1, keepdims=True)
    return yc * lax.rsqrt(var + LN_EPS) * g + b


def _gelu(x):
    return 0.5 * x * (1.0 + lax.erf(x * (1.0 / math.sqrt(2.0))))


def _matmul_kernel(x_ref, w_ref, o_ref):
    o_ref[...] = jnp.dot(x_ref[...], w_ref[...], preferred_element_type=F32).astype(o_ref.dtype)


def _matmul(x, w_stack, layer, out_dtype, tm=1024, tn=512):
    m, k = x.shape
    n = w_stack.shape[2]
    tm = min(tm, m)
    return pl.pallas_call(
        _matmul_kernel,
        grid=(m // tm, n // tn),
        in_specs=[pl.BlockSpec((tm, k), lambda i, j: (i, 0)),
                  pl.BlockSpec((None, k, tn), lambda i, j: (layer, 0, j))],
        out_specs=pl.BlockSpec((tm, tn), lambda i, j: (i, j)),
        out_shape=jax.ShapeDtypeStruct((m, n), out_dtype),
        compiler_params=_params("parallel", "parallel"),
        name="matmul",
    )(x, w_stack)


def _matmul_res_ln_kernel(a_ref, w_ref, x_ref, g_ref, b_ref, of_ref, ob_ref):
    h = jnp.dot(a_ref[...], w_ref[...], preferred_element_type=F32)
    out = _layer_norm(ALPHA * x_ref[...] + h, g_ref[...], b_ref[...])
    of_ref[...] = out
    ob_ref[...] = out.astype(BF16)


def _matmul_res_ln(a, w_stack, layer, x, g, b, tm=256):
    m, k = a.shape
    n = w_stack.shape[2]
    return pl.pallas_call(
        _matmul_res_ln_kernel,
        grid=(m // tm,),
        in_specs=[pl.BlockSpec((tm, k), lambda i: (i, 0)),
                  pl.BlockSpec((None, k, n), lambda i: (layer, 0, 0)),
                  pl.BlockSpec((tm, n), lambda i: (i, 0)),
                  pl.BlockSpec((1, n), lambda i: (0, 0)),
                  pl.BlockSpec((1, n), lambda i: (0, 0))],
        out_specs=[pl.BlockSpec((tm, n), lambda i: (i, 0)),
                   pl.BlockSpec((tm, n), lambda i: (i, 0))],
        out_shape=[jax.ShapeDtypeStruct((m, n), F32), jax.ShapeDtypeStruct((m, n), BF16)],
        compiler_params=_params("parallel"),
        name="matmul_res_ln",
    )(a, w_stack, x, g.reshape(1, n), b.reshape(1, n))


MIX_TC = 256
MIX_HALO = 16


def _mixer_kernel(gb_ref, gc_ref, xa_ref, gg_ref, xb_ref, ca_ref, cbw_ref, cbb_ref,
                  wr_ref, wi_ref, br_ref, bi_ref, lam_ref, y_ref, hf_scr):
    t_total = xb_ref.shape[0]
    n_chunks = t_total // MIX_TC
    n_ext = MIX_TC + 2 * MIX_HALO
    row = lax.broadcasted_iota(jnp.int32, (MIX_TC, LANES), 0)

    def ext(ref, c):
        t0 = pl.multiple_of(c * MIX_TC, MIX_TC)
        cur = ref[pl.ds(t0, MIX_TC), :].astype(F32)
        lo = pl.multiple_of(jnp.maximum(t0 - MIX_HALO, 0), MIX_HALO)
        hi = pl.multiple_of(jnp.minimum(t0 + MIX_TC, t_total - MIX_HALO), MIX_HALO)
        prev = jnp.where(c > 0, ref[pl.ds(lo, MIX_HALO), :].astype(F32), 0.0)
        nxt = jnp.where(c < n_chunks - 1, ref[pl.ds(hi, MIX_HALO), :].astype(F32), 0.0)
        return jnp.concatenate([prev, cur, nxt], axis=0)

    def shifted(xe, d):
        return pltpu.roll(xe, n_ext - MIX_HALO - d, 0)[:MIX_TC]

    def lru_inputs(c, direction):
        xe = ext(xb_ref, c)
        xc = jnp.zeros((MIX_TC, LANES), F32) + cbb_ref[direction:direction + 1, :]
        for k in range(CONV_B_W):
            d = k if direction == 1 else k - (CONV_B_W - 1)
            xc = xc + cbw_ref[direction, k:k + 1, :] * shifted(xe, d)
        xcb = xc.astype(BF16)
        r = jax.nn.sigmoid(jnp.dot(xcb, wr_ref[direction, 0], preferred_element_type=F32)
                           + br_ref[direction:direction + 1, :])
        i = jax.nn.sigmoid(jnp.dot(xcb, wi_ref[direction, 0], preferred_element_type=F32)
                           + bi_ref[direction:direction + 1, :])
        lam = lam_ref[direction:direction + 1, :]
        softplus_neg_lam = jnp.maximum(-lam, 0.0) + jnp.log1p(jnp.exp(-jnp.abs(lam)))
        log_a = -RG_C * r * softplus_neg_lam
        a = jnp.exp(log_a)
        u = jnp.sqrt(1.0 - jnp.exp(2.0 * log_a)) * i * xc
        return a, u

    def scan_chunk(a, u, carry, reverse):
        s = 1
        while s < MIX_TC:
            if reverse:
                keep = row < MIX_TC - s
                a_sh = jnp.where(keep, pltpu.roll(a, MIX_TC - s, 0), 1.0)
                u_sh = jnp.where(keep, pltpu.roll(u, MIX_TC - s, 0), 0.0)
            else:
                keep = row >= s
                a_sh = jnp.where(keep, pltpu.roll(a, s, 0), 1.0)
                u_sh = jnp.where(keep, pltpu.roll(u, s, 0), 0.0)
            u = a * u_sh + u
            a = a * a_sh
            s *= 2
        return u + a * carry

    def forward_body(c, carry):
        t0 = pl.multiple_of(c * MIX_TC, MIX_TC)
        pe = ext(gc_ref, c) * ext(xa_ref, c)
        conv = jnp.zeros((MIX_TC, LANES), F32)
        for k in range(CONV_A_W):
            conv = conv + ca_ref[k:k + 1, :] * shifted(pe, k - CONV_A_W // 2)
        y_a = gb_ref[pl.ds(t0, MIX_TC), :].astype(F32) * conv
        y_ref[pl.ds(t0, MIX_TC), 0:LANES] = y_a.astype(BF16)
        a, u = lru_inputs(c, 0)
        h = scan_chunk(a, u, carry, reverse=False)
        hf_scr[pl.ds(t0, MIX_TC), :] = h
        return h[MIX_TC - 1:MIX_TC, :]

    def backward_body(j, carry):
        c = n_chunks - 1 - j
        t0 = pl.multiple_of(c * MIX_TC, MIX_TC)
        a, u = lru_inputs(c, 1)
        h = scan_chunk(a, u, carry, reverse=True)
        gate = _gelu(gg_ref[pl.ds(t0, MIX_TC), :].astype(F32))
        y_b = gate * (hf_scr[pl.ds(t0, MIX_TC), :] + h)
        y_ref[pl.ds(t0, MIX_TC), LANES:2 * LANES] = y_b.astype(BF16)
        return h[0:1, :]

    zero = jnp.zeros((1, LANES), F32)
    lax.fori_loop(0, n_chunks, forward_body, zero)
    lax.fori_loop(0, n_chunks, backward_body, zero)


def _mixer(z, conv_a, conv_b_w, conv_b_b, w_r, b_r, w_i, b_i, lam):
    t = z.shape[0]
    ng = D_CONV // LANES

    def zcol(part):
        return pl.BlockSpec((t, LANES), lambda g, part=part: (0, part * ng + g))

    def vec(rows):
        return pl.BlockSpec((rows, LANES), lambda g: (0, g))

    gate_w = pl.BlockSpec((2, 1, LRU_HD, LRU_HD), lambda g: (0, g, 0, 0))
    return pl.pallas_call(
        _mixer_kernel,
        grid=(ng,),
        in_specs=[zcol(0), zcol(1), zcol(2), zcol(3), zcol(4),
                  vec(CONV_A_W),
                  pl.BlockSpec((2, CONV_B_W, LANES), lambda g: (0, 0, g)),
                  vec(2), gate_w, gate_w, vec(2), vec(2), vec(2)],
        out_specs=pl.BlockSpec((t, 2 * LANES), lambda g: (0, g)),
        out_shape=jax.ShapeDtypeStruct((t, D_MODEL), BF16),
        scratch_shapes=[pltpu.VMEM((t, LANES), F32)],
        compiler_params=_params("parallel"),
        name="mixer",
    )(z, z, z, z, z, conv_a, conv_b_w, conv_b_b, w_r.astype(BF16), w_i.astype(BF16), b_r, b_i, lam)


NA_QROWS = 8
NA_KROWS = 16
NA_HPS = 4


def _na_block_plan(rows, r0):
    kh = min(WIN_ROWS, rows)
    kr0 = min(max(r0 - WIN_ROWS // 2, 0), rows - NA_KROWS)
    plan = []
    for qi in range(NA_QROWS):
        qr = r0 + qi
        row_start = min(max(qr - kh // 2, 0), rows - kh)
        for pair in range(NA_KROWS // 2):
            kr = (kr0 + 2 * pair, kr0 + 2 * pair + 1)
            ok = [row_start <= r < row_start + kh for r in kr]
            d = [r - qr + (WIN_ROWS - 1) for r in kr]
            if ok[0] and ok[1]:
                plan.append((qi, pair, 0, d[0]))
            elif ok[0]:
                plan.append((qi, pair, 1, d[0]))
            elif ok[1]:
                plan.append((qi, pair, 2, d[1]))
    return plan


def _na_kernel(q_ref, k_ref, v_ref, tab_ref, o_ref, bias_scr):
    rows = k_ref.shape[0] // GRID_W
    nrb = rows // NA_QROWS
    rb = pl.program_id(1)

    def build(r0):
        bias_scr[...] = jnp.full(bias_scr.shape, MASK_VALUE, F32)
        for hh in range(NA_HPS):
            for qi, pair, kind, d in _na_block_plan(rows, r0):
                bias_scr[hh, qi * GRID_W:(qi + 1) * GRID_W, pair * LANES:(pair + 1) * LANES] = tab_ref[hh, kind, d]

    pl.when(rb == 0)(lambda: build(0))
    if nrb > 2:
        pl.when(rb == 1)(lambda: build(NA_QROWS))
    pl.when(rb == nrb - 1)(lambda: build(rows - NA_QROWS))

    kr0 = jnp.clip(rb * NA_QROWS - WIN_ROWS // 2, 0, rows - NA_KROWS)
    k0 = pl.multiple_of(kr0 * GRID_W, (WIN_ROWS // 2) * GRID_W)
    for hh in range(NA_HPS):
        cols = slice(hh * NA_HD, (hh + 1) * NA_HD)
        k = k_ref[pl.ds(k0, NA_KROWS * GRID_W), cols]
        v = v_ref[pl.ds(k0, NA_KROWS * GRID_W), cols]
        s = lax.dot_general(q_ref[:, cols], k, NT_DIMS, preferred_element_type=F32)
        s = s * (NA_HD ** -0.5) + bias_scr[hh]
        m = jnp.max(s, axis=-1, keepdims=True)
        p = jnp.exp(s - m)
        l = jnp.sum(p, axis=-1, keepdims=True)
        o = jnp.dot(p.astype(BF16), v, preferred_element_type=F32)
        o_ref[:, cols] = (o / l).astype(o_ref.dtype)


def _na_bias_table(rel_bias):
    c = np.arange(GRID_W)
    col_start = np.clip(c - WIN_COLS // 2, 0, GRID_W - WIN_COLS)
    col_ok = (c[None, :] >= col_start[:, None]) & (c[None, :] < col_start[:, None] + WIN_COLS)
    coff = np.clip(c[None, :] - c[:, None], -(WIN_COLS - 1), WIN_COLS - 1) + (WIN_COLS - 1)
    n_coff = 2 * WIN_COLS - 1
    col_onehot = (coff[None] == np.arange(n_coff)[:, None, None]).astype(np.float32)
    by_col = jnp.einsum("hdc,cqk->hdqk", rel_bias, col_onehot, precision=lax.Precision.HIGHEST)
    e = jnp.where(col_ok[None, None], by_col, MASK_VALUE)
    masked = jnp.full_like(e, MASK_VALUE)
    e_next = jnp.concatenate([e[:, 1:], masked[:, :1]], axis=1)
    return jnp.stack([jnp.concatenate([e, e_next], axis=-1),
                      jnp.concatenate([e, masked], axis=-1),
                      jnp.concatenate([masked, e], axis=-1)], axis=1)


def _neighbourhood_attention(qkv, rel_bias):
    t = qkv.shape[0]
    rows = t // GRID_W
    nrb = rows // NA_QROWS
    tq = NA_QROWS * GRID_W
    assert 2 * GRID_W == LANES and rows >= NA_KROWS
    table = _na_bias_table(rel_bias)
    ng = NA_HEADS // NA_HPS
    wide = NA_HPS * NA_HD
    resident = dict(pipeline_mode=pl.Buffered(1))
    return pl.pallas_call(
        _na_kernel,
        grid=(ng, nrb),
        in_specs=[pl.BlockSpec((tq, wide), lambda g, rb: (rb, g)),
                  pl.BlockSpec((t, wide), lambda g, rb: (0, ng + g), **resident),
                  pl.BlockSpec((t, wide), lambda g, rb: (0, 2 * ng + g), **resident),
                  pl.BlockSpec((NA_HPS,) + table.shape[1:], lambda g, rb: (g, 0, 0, 0, 0), **resident)],
        out_specs=pl.BlockSpec((tq, wide), lambda g, rb: (rb, g)),
        out_shape=jax.ShapeDtypeStruct((t, D_MODEL), BF16),
        scratch_shapes=[pltpu.VMEM((NA_HPS, tq, NA_KROWS * GRID_W), F32)],
        compiler_params=_params("parallel", "arbitrary"),
        name="neighbourhood_attention",
    )(qkv, qkv, qkv, table)


SEL_TT = 512
N_TOP = PEER_TOPK + 1
CAND_PAIRS = [(p, q) for p in range(N_TOP) for q in range(N_TOP // (p + 1))]
N_CAND_ROWS = -(-len(CAND_PAIRS) // 8) * 8


def _top_values(arrays, n, ranked):
    rows = [[] for _ in arrays]
    prev = [jnp.full((1, s.shape[1]), jnp.inf, F32) for s in arrays]
    rank = [jnp.zeros_like(s) if ranked[k] else None for k, s in enumerate(arrays)]
    for r in range(n):
        for k, s in enumerate(arrays):
            below = s < prev[k]
            if ranked[k] and r > 0:
                rank[k] = jnp.where(below, float(r), rank[k])
            prev[k] = jnp.max(jnp.where(below, s, NEG_INF), axis=0, keepdims=True)
            rows[k].append(prev[k])
    rank = [jnp.where(s < prev[k], float(n), rank[k]) if ranked[k] else None for k, s in enumerate(arrays)]
    return rows, rank


def _peer_select_kernel(xb_ref, wq_ref, keys_ref, cnt_ref, e1_ref, rank_ref, e2_ref, cand_scr):
    cand_scr[...] = jnp.full(cand_scr.shape, NEG_INF, F32)
    half = PEER_QDIM // 2

    def scores(pair):
        c0 = 2 * pair * PEER_QDIM
        q = jnp.dot(xb_ref[...], wq_ref[:, c0:c0 + 2 * PEER_QDIM], preferred_element_type=F32).astype(BF16)
        return [lax.dot_general(keys_ref[k % 2], q[:, k * half:(k + 1) * half], NT_DIMS,
                                preferred_element_type=F32) for k in range(4)]

    upcoming = scores(0)
    for h in range(PEER_HEADS):
        if h % 2 == 0:
            current = upcoming
            if h + 2 < PEER_HEADS:
                upcoming = scores(h // 2 + 1)
        s1, s2 = current[2 * (h % 2)], current[2 * (h % 2) + 1]
        (a, b), (_, rank) = _top_values([s1, s2], N_TOP, (False, True))
        for n, (p, q_) in enumerate(CAND_PAIRS):
            cand_scr[n:n + 1, :] = a[p] + b[q_]
        cand = cand_scr[...]
        (c,), _ = _top_values([cand], N_TOP, (False,))
        c_last, c_next = c[PEER_TOPK - 1], c[PEER_TOPK]
        thr = 0.5 * (c_last + c_next)
        m = a[0] + b[0]
        z = jnp.sum(jnp.where(cand >= c_last, jnp.exp(cand - m), 0.0), axis=0, keepdims=True)
        need = thr - s1
        cnt = jnp.zeros_like(s1)
        for q_ in range(N_TOP):
            cnt = jnp.where(b[q_] >= need, q_ + 1.0, cnt)
        cnt_ref[h] = cnt
        e1_ref[h] = jnp.exp(s1 - a[0]) / z
        rank_ref[h] = rank.astype(BF16)
        e2_ref[h] = jnp.exp(s2 - b[0]).astype(BF16)


def _peer_select(xb, w_q_stack, sub_keys_stack, layer):
    t, d = xb.shape
    shape = (PEER_HEADS, N_KEYS, t)
    out_spec = pl.BlockSpec((PEER_HEADS, N_KEYS, SEL_TT), lambda i: (0, 0, i))
    return pl.pallas_call(
        _peer_select_kernel,
        grid=(t // SEL_TT,),
        in_specs=[pl.BlockSpec((SEL_TT, d), lambda i: (i, 0)),
                  pl.BlockSpec((None, d, PEER_HEADS * PEER_QDIM), lambda i: (layer, 0, 0)),
                  pl.BlockSpec((None, 2, N_KEYS, PEER_QDIM // 2), lambda i: (layer, 0, 0, 0))],
        out_specs=[out_spec] * 4,
        out_shape=[jax.ShapeDtypeStruct(shape, F32), jax.ShapeDtypeStruct(shape, F32),
                   jax.ShapeDtypeStruct(shape, BF16), jax.ShapeDtypeStruct(shape, BF16)],
        scratch_shapes=[pltpu.VMEM((N_CAND_ROWS, SEL_TT), F32)],
        compiler_params=_params("parallel"),
        name="peer_select",
    )(xb, w_q_stack, sub_keys_stack)


PEER_TT = 512
PEER_TE = 1024
PEER_C1 = 256
PEER_C2 = 512
PEER_D2 = 512
SUB16 = 16
assert PEER_TE == 4 * PEER_C1 == 2 * PEER_C2 and D_MODEL == 4 * PEER_D2


def _gated_activation(h_blk, i, cnt_ref, e1_ref, rank_ref, e2_ref, a_scr, row0):
    tt = h_blk.shape[1]
    n_jb = N_KEYS // SUB16
    gates = [None] * n_jb
    for h in range(PEER_HEADS):
        cnt_i = jnp.broadcast_to(cnt_ref[h, pl.ds(i, 1), :], (SUB16, tt)).astype(BF16)
        e1_i = jnp.broadcast_to(e1_ref[h, pl.ds(i, 1), :], (SUB16, tt)).astype(BF16)
        for jb in range(n_jb):
            rows = slice(jb * SUB16, (jb + 1) * SUB16)
            term = jnp.where(rank_ref[h, rows, :] < cnt_i, e2_ref[h, rows, :], 0) * e1_i
            gates[jb] = term if h == 0 else gates[jb] + term
    for jb in range(n_jb):
        rows = slice(jb * SUB16, (jb + 1) * SUB16)
        act = _gelu(h_blk[rows])
        a_scr[row0 + jb * SUB16:row0 + (jb + 1) * SUB16, :] = act.astype(BF16) * gates[jb]


def _peer_expert_kernel(x_ref, u_ref, vt_ref, cnt_ref, e1_ref, rank_ref, e2_ref, g_ref, beta_ref,
                        of_ref, ob_ref, xb_scr, acc_scr, a_scr):
    e = pl.program_id(1)
    n_i = PEER_TE // N_KEYS

    @pl.when(e == 0)
    def _():
        xb_scr[...] = x_ref[...].astype(BF16)
        acc_scr[...] = jnp.zeros(acc_scr.shape, F32)

    xb = xb_scr[...]

    def first_matmul(c):
        r0 = c * PEER_C1
        return lax.dot_general(u_ref[r0:r0 + PEER_C1, :], xb, NT_DIMS, preferred_element_type=F32)

    def activation(ii, h_t):
        r = (ii * N_KEYS) % PEER_C1
        _gated_activation(h_t[r:r + N_KEYS], e * n_i + ii, cnt_ref, e1_ref, rank_ref, e2_ref, a_scr, ii * N_KEYS)

    def second_matmul(c2, piece):
        r0, k0 = piece * PEER_D2, c2 * PEER_C2
        acc_scr[r0:r0 + PEER_D2, :] += jnp.dot(vt_ref[r0:r0 + PEER_D2, k0:k0 + PEER_C2], a_scr[k0:k0 + PEER_C2, :],
                                               preferred_element_type=F32)

    h0 = first_matmul(0)
    h1 = first_matmul(1)
    activation(0, h0)
    activation(1, h0)
    h2 = first_matmul(2)
    activation(2, h1)
    activation(3, h1)
    h3 = first_matmul(3)
    activation(4, h2)
    second_matmul(0, 0)
    activation(5, h2)
    second_matmul(0, 1)
    activation(6, h3)
    second_matmul(0, 2)
    activation(7, h3)
    second_matmul(0, 3)
    for piece in range(acc_scr.shape[0] // PEER_D2):
        second_matmul(1, piece)

    @pl.when(e == pl.num_programs(1) - 1)
    def _():
        y = ALPHA * x_ref[...] + acc_scr[...].T
        out = _layer_norm(y, g_ref[...], beta_ref[...])
        of_ref[...] = out
        ob_ref[...] = out.astype(BF16)


def _peer_experts(x, u_stack, vt_stack, layer, sel, g, b):
    t, d = x.shape
    n_e = u_stack.shape[1]
    resident = dict(pipeline_mode=pl.Buffered(1))
    sel_spec = pl.BlockSpec((PEER_HEADS, N_KEYS, PEER_TT), lambda i, e: (0, 0, i), **resident)
    return pl.pallas_call(
        _peer_expert_kernel,
        grid=(t // PEER_TT, n_e // PEER_TE),
        in_specs=[pl.BlockSpec((PEER_TT, d), lambda i, e: (i, 0), **resident),
                  pl.BlockSpec((None, PEER_TE, d), lambda i, e: (layer, e, 0)),
                  pl.BlockSpec((None, d, PEER_TE), lambda i, e: (layer, 0, e)),
                  sel_spec, sel_spec, sel_spec, sel_spec,
                  pl.BlockSpec((1, d), lambda i, e: (0, 0)),
                  pl.BlockSpec((1, d), lambda i, e: (0, 0))],
        out_specs=[pl.BlockSpec((PEER_TT, d), lambda i, e: (i, 0)),
                   pl.BlockSpec((PEER_TT, d), lambda i, e: (i, 0))],
        out_shape=[jax.ShapeDtypeStruct((t, d), F32), jax.ShapeDtypeStruct((t, d), BF16)],
        scratch_shapes=[pltpu.VMEM((PEER_TT, d), BF16),
                        pltpu.VMEM((d, PEER_TT), F32),
                        pltpu.VMEM((PEER_TE, PEER_TT), BF16)],
        compiler_params=_params("parallel", "arbitrary"),
        name="peer_experts",
    )(x, u_stack, vt_stack, *sel, g.reshape(1, d), b.reshape(1, d))


TRANSPOSE_TE = 1024


def _transpose_cast_kernel(x_ref, o_ref):
    o_ref[...] = x_ref[...].T.astype(o_ref.dtype)


def _transpose_cast(x):
    n, e, d = x.shape
    return pl.pallas_call(
        _transpose_cast_kernel,
        grid=(n, e // TRANSPOSE_TE),
        in_specs=[pl.BlockSpec((None, TRANSPOSE_TE, d), lambda l, j: (l, j, 0))],
        out_specs=pl.BlockSpec((None, d, TRANSPOSE_TE), lambda l, j: (l, 0, j)),
        out_shape=jax.ShapeDtypeStruct((n, d, e), BF16),
        compiler_params=_params("parallel", "parallel"),
        name="transpose_cast",
    )(x)


def _interleave_groups(w_out):
    ng = D_CONV // LANES
    n = w_out.shape[0]
    return w_out.reshape(n, 2, ng, LANES, w_out.shape[-1]).transpose(0, 2, 1, 3, 4).reshape(w_out.shape)


def kernel(x, ev_w_in, ev_conv_a, ev_conv_b_w, ev_conv_b_b, ev_w_r, ev_b_r, ev_w_i, ev_b_i, ev_lam, ev_w_out, od_w_qkv, od_rel_bias, od_w_o, pk_w_q, pk_sub_keys, pk_u, pk_v, ln1_g, ln1_b, ln2_g, ln2_b):
    bsz, seq, d = x.shape
    w_in = ev_w_in.astype(BF16)
    w_out = _interleave_groups(ev_w_out).astype(BF16)
    w_qkv = od_w_qkv.astype(BF16)
    w_o = od_w_o.astype(BF16)
    w_q = pk_w_q.astype(BF16)
    sub_keys = pk_sub_keys.astype(BF16)
    u = pk_u.astype(BF16)
    vt = _transpose_cast(pk_v)
    outs = []
    for bi in range(bsz):
        xf = x[bi]
        xb = xf.astype(BF16)
        for l in range(DEPTH):
            i = l // 2
            if l % 2 == 0:
                z = _matmul(xb, w_in, i, BF16)
                y = _mixer(z, ev_conv_a[i], ev_conv_b_w[i], ev_conv_b_b[i], ev_w_r[i], ev_b_r[i],
                           ev_w_i[i], ev_b_i[i], ev_lam[i])
                w_mix = w_out
            else:
                qkv = _matmul(xb, w_qkv, i, BF16)
                y = _neighbourhood_attention(qkv, od_rel_bias[i])
                w_mix = w_o
            xf, xb = _matmul_res_ln(y, w_mix, i, xf, ln1_g[l], ln1_b[l])
            sel = _peer_select(xb, w_q, sub_keys, l)
            xf, xb = _peer_experts(xf, u, vt, l, sel, ln2_g[l], ln2_b[l])
        outs.append(xf)
    return jnp.stack(outs)
```

```python
import math

import numpy as np
import jax
import jax.numpy as jnp
from jax import lax
from jax.experimental import pallas as pl
from jax.experimental.pallas import tpu as pltpu

F32 = jnp.float32
BF16 = jnp.bfloat16

D_MODEL = 2048
DEPTH = 4
GRID_W = 64
D_CONV = D_MODEL // 2
D_LRU = D_MODEL // 2
LRU_HEADS = 8
LRU_HD = D_LRU // LRU_HEADS
CONV_A_W = 3
CONV_B_W = 4
RG_C = 8.0
NA_HEADS = 16
NA_HD = D_MODEL // NA_HEADS
WIN_ROWS = 8
WIN_COLS = 16
PEER_HEADS = 8
PEER_QDIM = 256
N_KEYS = 128
PEER_TOPK = 16
ALPHA = (2 * DEPTH) ** 0.25
LN_EPS = 1e-5

LANES = 128
VMEM_LIMIT = 56 * 1024 * 1024
NEG_INF = float("-inf")
MASK_VALUE = -1e30

NT_DIMS = (((1,), (1,)), ((), ()))


def _params(*semantics):
    return pltpu.CompilerParams(dimension_semantics=semantics, vmem_limit_bytes=VMEM_LIMIT)


def _layer_norm(y, g, b):
    mu = jnp.mean(y, axis=-1, keepdims=True)
    yc = y - mu
    var = jnp.mean(yc * yc, axis=-1, keepdims=True)
    return yc * lax.rsqrt(var + LN_EPS) * g + b


def _gelu(x):
    return 0.5 * x * (1.0 + lax.erf(x * (1.0 / math.sqrt(2.0))))


def _matmul_kernel(x_ref, w_ref, o_ref):
    o_ref[...] = jnp.dot(x_ref[...], w_ref[...], preferred_element_type=F32).astype(o_ref.dtype)


def _matmul(x, w_stack, layer, out_dtype, tm=1024, tn=512):
    m, k = x.shape
    n = w_stack.shape[2]
    tm = min(tm, m)
    return pl.pallas_call(
        _matmul_kernel,
        grid=(m // tm, n // tn),
        in_specs=[pl.BlockSpec((tm, k), lambda i, j: (i, 0)),
                  pl.BlockSpec((None, k, tn), lambda i, j: (layer, 0, j))],
        out_specs=pl.BlockSpec((tm, tn), lambda i, j: (i, j)),
        out_shape=jax.ShapeDtypeStruct((m, n), out_dtype),
        compiler_params=_params("parallel", "parallel"),
        name="matmul",
    )(x, w_stack)


RES_LN_SUB = 256


def _matmul_res_ln_kernel(a_ref, w_ref, x_ref, g_ref, b_ref, of_ref, ob_ref):
    for r0 in range(0, a_ref.shape[0], RES_LN_SUB):
        rows = slice(r0, r0 + RES_LN_SUB)
        h = jnp.dot(a_ref[rows, :], w_ref[...], preferred_element_type=F32)
        out = _layer_norm(ALPHA * x_ref[rows, :] + h, g_ref[...], b_ref[...])
        of_ref[rows, :] = out
        ob_ref[rows, :] = out.astype(BF16)


def _matmul_res_ln(a, w_stack, layer, x, g, b, tm=2 * RES_LN_SUB):
    m, k = a.shape
    n = w_stack.shape[2]
    return pl.pallas_call(
        _matmul_res_ln_kernel,
        grid=(m // tm,),
        in_specs=[pl.BlockSpec((tm, k), lambda i: (i, 0)),
                  pl.BlockSpec((None, k, n), lambda i: (layer, 0, 0)),
                  pl.BlockSpec((tm, n), lambda i: (i, 0)),
                  pl.BlockSpec((1, n), lambda i: (0, 0)),
                  pl.BlockSpec((1, n), lambda i: (0, 0))],
        out_specs=[pl.BlockSpec((tm, n), lambda i: (i, 0)),
                   pl.BlockSpec((tm, n), lambda i: (i, 0))],
        out_shape=[jax.ShapeDtypeStruct((m, n), F32), jax.ShapeDtypeStruct((m, n), BF16)],
        compiler_params=_params("parallel"),
        name="matmul_res_ln",
    )(a, w_stack, x, g.reshape(1, n), b.reshape(1, n))


MIX_TC = 256
MIX_HALO = 16


def _mixer_kernel(gb_ref, gc_ref, xa_ref, gg_ref, xb_ref, ca_ref, cbw_ref, cbb_ref,
                  wr_ref, wi_ref, br_ref, bi_ref, lam_ref, y_ref, hf_scr):
    t_total = xb_ref.shape[0]
    n_chunks = t_total // MIX_TC
    n_ext = MIX_TC + 2 * MIX_HALO
    row = lax.broadcasted_iota(jnp.int32, (MIX_TC, LANES), 0)

    def ext(ref, c):
        t0 = pl.multiple_of(c * MIX_TC, MIX_TC)
        cur = ref[pl.ds(t0, MIX_TC), :].astype(F32)
        lo = pl.multiple_of(jnp.maximum(t0 - MIX_HALO, 0), MIX_HALO)
        hi = pl.multiple_of(jnp.minimum(t0 + MIX_TC, t_total - MIX_HALO), MIX_HALO)
        prev = jnp.where(c > 0, ref[pl.ds(lo, MIX_HALO), :].astype(F32), 0.0)
        nxt = jnp.where(c < n_chunks - 1, ref[pl.ds(hi, MIX_HALO), :].astype(F32), 0.0)
        return jnp.concatenate([prev, cur, nxt], axis=0)

    def shifted(xe, d):
        return pltpu.roll(xe, n_ext - MIX_HALO - d, 0)[:MIX_TC]

    def lru_inputs(c, direction):
        xe = ext(xb_ref, c)
        xc = jnp.zeros((MIX_TC, LANES), F32) + cbb_ref[direction:direction + 1, :]
        for k in range(CONV_B_W):
            d = k if direction == 1 else k - (CONV_B_W - 1)
            xc = xc + cbw_ref[direction, k:k + 1, :] * shifted(xe, d)
        xcb = xc.astype(BF16)
        r = jax.nn.sigmoid(jnp.dot(xcb, wr_ref[direction, 0], preferred_element_type=F32)
                           + br_ref[direction:direction + 1, :])
        i = jax.nn.sigmoid(jnp.dot(xcb, wi_ref[direction, 0], preferred_element_type=F32)
                           + bi_ref[direction:direction + 1, :])
        lam = lam_ref[direction:direction + 1, :]
        softplus_neg_lam = jnp.maximum(-lam, 0.0) + jnp.log1p(jnp.exp(-jnp.abs(lam)))
        log_a = -RG_C * r * softplus_neg_lam
        a = jnp.exp(log_a)
        u = jnp.sqrt(1.0 - jnp.exp(2.0 * log_a)) * i * xc
        return a, u

    def scan_chunk(a, u, carry, reverse):
        s = 1
        while s < MIX_TC:
            if reverse:
                keep = row < MIX_TC - s
                a_sh = jnp.where(keep, pltpu.roll(a, MIX_TC - s, 0), 1.0)
                u_sh = jnp.where(keep, pltpu.roll(u, MIX_TC - s, 0), 0.0)
            else:
                keep = row >= s
                a_sh = jnp.where(keep, pltpu.roll(a, s, 0), 1.0)
                u_sh = jnp.where(keep, pltpu.roll(u, s, 0), 0.0)
            u = a * u_sh + u
            a = a * a_sh
            s *= 2
        return u + a * carry

    def forward_body(c, carry):
        t0 = pl.multiple_of(c * MIX_TC, MIX_TC)
        pe = ext(gc_ref, c) * ext(xa_ref, c)
        conv = jnp.zeros((MIX_TC, LANES), F32)
        for k in range(CONV_A_W):
            conv = conv + ca_ref[k:k + 1, :] * shifted(pe, k - CONV_A_W // 2)
        y_a = gb_ref[pl.ds(t0, MIX_TC), :].astype(F32) * conv
        y_ref[pl.ds(t0, MIX_TC), 0:LANES] = y_a.astype(BF16)
        a, u = lru_inputs(c, 0)
        h = scan_chunk(a, u, carry, reverse=False)
        hf_scr[pl.ds(t0, MIX_TC), :] = h
        return h[MIX_TC - 1:MIX_TC, :]

    def backward_body(j, carry):
        c = n_chunks - 1 - j
        t0 = pl.multiple_of(c * MIX_TC, MIX_TC)
        a, u = lru_inputs(c, 1)
        h = scan_chunk(a, u, carry, reverse=True)
        gate = _gelu(gg_ref[pl.ds(t0, MIX_TC), :].astype(F32))
        y_b = gate * (hf_scr[pl.ds(t0, MIX_TC), :] + h)
        y_ref[pl.ds(t0, MIX_TC), LANES:2 * LANES] = y_b.astype(BF16)
        return h[0:1, :]

    zero = jnp.zeros((1, LANES), F32)
    lax.fori_loop(0, n_chunks, forward_body, zero)
    lax.fori_loop(0, n_chunks, backward_body, zero)


def _mixer(z, conv_a, conv_b_w, conv_b_b, w_r, b_r, w_i, b_i, lam):
    t = z.shape[0]
    ng = D_CONV // LANES

    def zcol(part):
        return pl.BlockSpec((t, LANES), lambda g, part=part: (0, part * ng + g))

    def vec(rows):
        return pl.BlockSpec((rows, LANES), lambda g: (0, g))

    gate_w = pl.BlockSpec((2, 1, LRU_HD, LRU_HD), lambda g: (0, g, 0, 0))
    return pl.pallas_call(
        _mixer_kernel,
        grid=(ng,),
        in_specs=[zcol(0), zcol(1), zcol(2), zcol(3), zcol(4),
                  vec(CONV_A_W),
                  pl.BlockSpec((2, CONV_B_W, LANES), lambda g: (0, 0, g)),
                  vec(2), gate_w, gate_w, vec(2), vec(2), vec(2)],
        out_specs=pl.BlockSpec((t, 2 * LANES), lambda g: (0, g)),
        out_shape=jax.ShapeDtypeStruct((t, D_MODEL), BF16),
        scratch_shapes=[pltpu.VMEM((t, LANES), F32)],
        compiler_params=_params("parallel"),
        name="mixer",
    )(z, z, z, z, z, conv_a, conv_b_w, conv_b_b, w_r.astype(BF16), w_i.astype(BF16), b_r, b_i, lam)


NA_QROWS = 8
NA_KROWS = 16
NA_HPS = 4


def _na_block_plan(rows, r0):
    kh = min(WIN_ROWS, rows)
    kr0 = min(max(r0 - WIN_ROWS // 2, 0), rows - NA_KROWS)
    plan = []
    for qi in range(NA_QROWS):
        qr = r0 + qi
        row_start = min(max(qr - kh // 2, 0), rows - kh)
        for pair in range(NA_KROWS // 2):
            kr = (kr0 + 2 * pair, kr0 + 2 * pair + 1)
            ok = [row_start <= r < row_start + kh for r in kr]
            d = [r - qr + (WIN_ROWS - 1) for r in kr]
            if ok[0] and ok[1]:
                plan.append((qi, pair, 0, d[0]))
            elif ok[0]:
                plan.append((qi, pair, 1, d[0]))
            elif ok[1]:
                plan.append((qi, pair, 2, d[1]))
    return plan


def _na_kernel(q_ref, k_ref, v_ref, tab_ref, o_ref, bias_scr):
    rows = k_ref.shape[0] // GRID_W
    nrb = rows // NA_QROWS
    rb = pl.program_id(1)

    def build(r0):
        bias_scr[...] = jnp.full(bias_scr.shape, MASK_VALUE, F32)
        for hh in range(NA_HPS):
            for qi, pair, kind, d in _na_block_plan(rows, r0):
                bias_scr[hh, qi * GRID_W:(qi + 1) * GRID_W, pair * LANES:(pair + 1) * LANES] = tab_ref[hh, kind, d]

    pl.when(rb == 0)(lambda: build(0))
    if nrb > 2:
        pl.when(rb == 1)(lambda: build(NA_QROWS))
    pl.when(rb == nrb - 1)(lambda: build(rows - NA_QROWS))

    kr0 = jnp.clip(rb * NA_QROWS - WIN_ROWS // 2, 0, rows - NA_KROWS)
    k0 = pl.multiple_of(kr0 * GRID_W, (WIN_ROWS // 2) * GRID_W)
    for hh in range(NA_HPS):
        cols = slice(hh * NA_HD, (hh + 1) * NA_HD)
        k = k_ref[pl.ds(k0, NA_KROWS * GRID_W), cols]
        v = v_ref[pl.ds(k0, NA_KROWS * GRID_W), cols]
        s = lax.dot_general(q_ref[:, cols], k, NT_DIMS, preferred_element_type=F32)
        s = s * (NA_HD ** -0.5) + bias_scr[hh]
        m = jnp.max(s, axis=-1, keepdims=True)
        p = jnp.exp(s - m)
        l = jnp.sum(p, axis=-1, keepdims=True)
        o = jnp.dot(p.astype(BF16), v, preferred_element_type=F32)
        o_ref[:, cols] = (o / l).astype(o_ref.dtype)


def _na_bias_table(rel_bias):
    c = np.arange(GRID_W)
    col_start = np.clip(c - WIN_COLS // 2, 0, GRID_W - WIN_COLS)
    col_ok = (c[None, :] >= col_start[:, None]) & (c[None, :] < col_start[:, None] + WIN_COLS)
    coff = np.clip(c[None, :] - c[:, None], -(WIN_COLS - 1), WIN_COLS - 1) + (WIN_COLS - 1)
    n_coff = 2 * WIN_COLS - 1
    col_onehot = (coff[None] == np.arange(n_coff)[:, None, None]).astype(np.float32)
    by_col = jnp.einsum("hdc,cqk->hdqk", rel_bias, col_onehot, precision=lax.Precision.HIGHEST)
    e = jnp.where(col_ok[None, None], by_col, MASK_VALUE)
    masked = jnp.full_like(e, MASK_VALUE)
    e_next = jnp.concatenate([e[:, 1:], masked[:, :1]], axis=1)
    return jnp.stack([jnp.concatenate([e, e_next], axis=-1),
                      jnp.concatenate([e, masked], axis=-1),
                      jnp.concatenate([masked, e], axis=-1)], axis=1)


def _neighbourhood_attention(qkv, rel_bias):
    t = qkv.shape[0]
    rows = t // GRID_W
    nrb = rows // NA_QROWS
    tq = NA_QROWS * GRID_W
    assert 2 * GRID_W == LANES and rows >= NA_KROWS
    table = _na_bias_table(rel_bias)
    ng = NA_HEADS // NA_HPS
    wide = NA_HPS * NA_HD
    resident = dict(pipeline_mode=pl.Buffered(1))
    return pl.pallas_call(
        _na_kernel,
        grid=(ng, nrb),
        in_specs=[pl.BlockSpec((tq, wide), lambda g, rb: (rb, g)),
                  pl.BlockSpec((t, wide), lambda g, rb: (0, ng + g), **resident),
                  pl.BlockSpec((t, wide), lambda g, rb: (0, 2 * ng + g), **resident),
                  pl.BlockSpec((NA_HPS,) + table.shape[1:], lambda g, rb: (g, 0, 0, 0, 0), **resident)],
        out_specs=pl.BlockSpec((tq, wide), lambda g, rb: (rb, g)),
        out_shape=jax.ShapeDtypeStruct((t, D_MODEL), BF16),
        scratch_shapes=[pltpu.VMEM((NA_HPS, tq, NA_KROWS * GRID_W), F32)],
        compiler_params=_params("parallel", "arbitrary"),
        name="neighbourhood_attention",
    )(qkv, qkv, qkv, table)


SEL_TT = 512
N_TOP = PEER_TOPK + 1
CAND_PAIRS = [(p, q) for p in range(N_TOP) for q in range(N_TOP // (p + 1))]
N_CAND_ROWS = -(-len(CAND_PAIRS) // 8) * 8


def _top_values(arrays, n, ranked):
    rows = [[] for _ in arrays]
    prev = [jnp.full((1, s.shape[1]), jnp.inf, F32) for s in arrays]
    rank = [jnp.zeros_like(s) if ranked[k] else None for k, s in enumerate(arrays)]
    for r in range(n):
        for k, s in enumerate(arrays):
            below = s < prev[k]
            if ranked[k] and r > 0:
                rank[k] = jnp.where(below, float(r), rank[k])
            prev[k] = jnp.max(jnp.where(below, s, NEG_INF), axis=0, keepdims=True)
            rows[k].append(prev[k])
    rank = [jnp.where(s < prev[k], float(n), rank[k]) if ranked[k] else None for k, s in enumerate(arrays)]
    return rows, rank


def _peer_select_kernel(xb_ref, wq_ref, keys_ref, cnt_ref, e1_ref, rank_ref, e2_ref, cand_scr):
    cand_scr[...] = jnp.full(cand_scr.shape, NEG_INF, F32)
    half = PEER_QDIM // 2

    def scores(pair):
        c0 = 2 * pair * PEER_QDIM
        q = jnp.dot(xb_ref[...], wq_ref[:, c0:c0 + 2 * PEER_QDIM], preferred_element_type=F32).astype(BF16)
        return [lax.dot_general(keys_ref[k % 2], q[:, k * half:(k + 1) * half], NT_DIMS,
                                preferred_element_type=F32) for k in range(4)]

    upcoming = scores(0)
    for h in range(PEER_HEADS):
        if h % 2 == 0:
            current = upcoming
            if h + 2 < PEER_HEADS:
                upcoming = scores(h // 2 + 1)
        s1, s2 = current[2 * (h % 2)], current[2 * (h % 2) + 1]
        (a, b), (_, rank) = _top_values([s1, s2], N_TOP, (False, True))
        for n, (p, q_) in enumerate(CAND_PAIRS):
            cand_scr[n:n + 1, :] = a[p] + b[q_]
        cand = cand_scr[...]
        (c,), _ = _top_values([cand], N_TOP, (False,))
        c_last, c_next = c[PEER_TOPK - 1], c[PEER_TOPK]
        thr = 0.5 * (c_last + c_next)
        m = a[0] + b[0]
        z = jnp.sum(jnp.where(cand >= c_last, jnp.exp(cand - m), 0.0), axis=0, keepdims=True)
        need = thr - s1
        cnt = jnp.zeros_like(s1)
        for q_ in range(N_TOP):
            cnt = jnp.where(b[q_] >= need, q_ + 1.0, cnt)
        cnt_ref[h] = cnt
        e1_ref[h] = jnp.exp(s1 - a[0]) / z
        rank_ref[h] = rank.astype(BF16)
        e2_ref[h] = jnp.exp(s2 - b[0]).astype(BF16)


def _peer_select(xb, w_q_stack, sub_keys_stack, layer):
    t, d = xb.shape
    shape = (PEER_HEADS, N_KEYS, t)
    out_spec = pl.BlockSpec((PEER_HEADS, N_KEYS, SEL_TT), lambda i: (0, 0, i))
    return pl.pallas_call(
        _peer_select_kernel,
        grid=(t // SEL_TT,),
        in_specs=[pl.BlockSpec((SEL_TT, d), lambda i: (i, 0)),
                  pl.BlockSpec((None, d, PEER_HEADS * PEER_QDIM), lambda i: (layer, 0, 0)),
                  pl.BlockSpec((None, 2, N_KEYS, PEER_QDIM // 2), lambda i: (layer, 0, 0, 0))],
        out_specs=[out_spec] * 4,
        out_shape=[jax.ShapeDtypeStruct(shape, F32), jax.ShapeDtypeStruct(shape, F32),
                   jax.ShapeDtypeStruct(shape, BF16), jax.ShapeDtypeStruct(shape, BF16)],
        scratch_shapes=[pltpu.VMEM((N_CAND_ROWS, SEL_TT), F32)],
        compiler_params=_params("parallel"),
        name="peer_select",
    )(xb, w_q_stack, sub_keys_stack)


PEER_TT = 512
PEER_TE = 1024
PEER_C1 = 256
PEER_C2 = 512
PEER_D2 = 512
SUB16 = 16
PEER_VMEM_LIMIT = 60 * 1024 * 1024
assert PEER_TE == 4 * PEER_C1 == 2 * PEER_C2 and D_MODEL == 4 * PEER_D2


def _gated_activation(h_blk, i, cnt_ref, e1_ref, rank_ref, e2_ref, a_scr, row0):
    tt = h_blk.shape[1]
    n_jb = N_KEYS // SUB16
    gates = [None] * n_jb
    for h in range(PEER_HEADS):
        cnt_i = jnp.broadcast_to(cnt_ref[h, pl.ds(i, 1), :], (SUB16, tt)).astype(BF16)
        e1_i = jnp.broadcast_to(e1_ref[h, pl.ds(i, 1), :], (SUB16, tt)).astype(BF16)
        for jb in range(n_jb):
            rows = slice(jb * SUB16, (jb + 1) * SUB16)
            term = jnp.where(rank_ref[h, rows, :] < cnt_i, e2_ref[h, rows, :], 0) * e1_i
            gates[jb] = term if h == 0 else gates[jb] + term
    for jb in range(n_jb):
        rows = slice(jb * SUB16, (jb + 1) * SUB16)
        act = _gelu(h_blk[rows])
        a_scr[row0 + jb * SUB16:row0 + (jb + 1) * SUB16, :] = act.astype(BF16) * gates[jb]


def _peer_expert_kernel(x_ref, u_ref, vt_ref, cnt_ref, e1_ref, rank_ref, e2_ref, g_ref, beta_ref,
                        of_ref, ob_ref, xb_scr, acc_scr, a_scr):
    e = pl.program_id(1)
    n_i = PEER_TE // N_KEYS

    @pl.when(e == 0)
    def _():
        xb_scr[...] = x_ref[...].astype(BF16)
        acc_scr[...] = jnp.zeros(acc_scr.shape, F32)

    xb = xb_scr[...]

    def first_matmul(c):
        r0 = c * PEER_C1
        return lax.dot_general(u_ref[r0:r0 + PEER_C1, :], xb, NT_DIMS, preferred_element_type=F32)

    def activation(ii, h_t):
        r = (ii * N_KEYS) % PEER_C1
        _gated_activation(h_t[r:r + N_KEYS], e * n_i + ii, cnt_ref, e1_ref, rank_ref, e2_ref, a_scr, ii * N_KEYS)

    def second_matmul(c2, piece):
        r0, k0 = piece * PEER_D2, c2 * PEER_C2
        acc_scr[r0:r0 + PEER_D2, :] += jnp.dot(vt_ref[r0:r0 + PEER_D2, k0:k0 + PEER_C2], a_scr[k0:k0 + PEER_C2, :],
                                               preferred_element_type=F32)

    h0 = first_matmul(0)
    h1 = first_matmul(1)
    activation(0, h0)
    activation(1, h0)
    h2 = first_matmul(2)
    activation(2, h1)
    activation(3, h1)
    h3 = first_matmul(3)
    activation(4, h2)
    second_matmul(0, 0)
    activation(5, h2)
    second_matmul(0, 1)
    activation(6, h3)
    second_matmul(0, 2)
    activation(7, h3)
    second_matmul(0, 3)
    for piece in range(acc_scr.shape[0] // PEER_D2):
        second_matmul(1, piece)

    @pl.when(e == pl.num_programs(1) - 1)
    def _():
        y = ALPHA * x_ref[...] + acc_scr[...].T
        out = _layer_norm(y, g_ref[...], beta_ref[...])
        of_ref[...] = out
        ob_ref[...] = out.astype(BF16)


def _peer_experts(x, u_stack, vt_stack, layer, sel, g, b):
    t, d = x.shape
    n_e = u_stack.shape[1]
    sel_spec = pl.BlockSpec((PEER_HEADS, N_KEYS, PEER_TT), lambda i, e: (0, 0, i))
    return pl.pallas_call(
        _peer_expert_kernel,
        grid=(t // PEER_TT, n_e // PEER_TE),
        in_specs=[pl.BlockSpec((PEER_TT, d), lambda i, e: (i, 0), pipeline_mode=pl.Buffered(1)),
                  pl.BlockSpec((None, PEER_TE, d), lambda i, e: (layer, e, 0)),
                  pl.BlockSpec((None, d, PEER_TE), lambda i, e: (layer, 0, e)),
                  sel_spec, sel_spec, sel_spec, sel_spec,
                  pl.BlockSpec((1, d), lambda i, e: (0, 0)),
                  pl.BlockSpec((1, d), lambda i, e: (0, 0))],
        out_specs=[pl.BlockSpec((PEER_TT, d), lambda i, e: (i, 0)),
                   pl.BlockSpec((PEER_TT, d), lambda i, e: (i, 0))],
        out_shape=[jax.ShapeDtypeStruct((t, d), F32), jax.ShapeDtypeStruct((t, d), BF16)],
        scratch_shapes=[pltpu.VMEM((PEER_TT, d), BF16),
                        pltpu.VMEM((d, PEER_TT), F32),
                        pltpu.VMEM((PEER_TE, PEER_TT), BF16)],
        compiler_params=pltpu.CompilerParams(dimension_semantics=("parallel", "arbitrary"),
                                             vmem_limit_bytes=PEER_VMEM_LIMIT),
        name="peer_experts",
    )(x, u_stack, vt_stack, *sel, g.reshape(1, d), b.reshape(1, d))


TRANSPOSE_TE = 1024


def _transpose_cast_kernel(x_ref, o_ref):
    o_ref[...] = x_ref[...].T.astype(o_ref.dtype)


def _transpose_cast(x):
    n, e, d = x.shape
    return pl.pallas_call(
        _transpose_cast_kernel,
        grid=(n, e // TRANSPOSE_TE),
        in_specs=[pl.BlockSpec((None, TRANSPOSE_TE, d), lambda l, j: (l, j, 0))],
        out_specs=pl.BlockSpec((None, d, TRANSPOSE_TE), lambda l, j: (l, 0, j)),
        out_shape=jax.ShapeDtypeStruct((n, d, e), BF16),
        compiler_params=_params("parallel", "parallel"),
        name="transpose_cast",
    )(x)


def _interleave_groups(w_out):
    ng = D_CONV // LANES
    n = w_out.shape[0]
    return w_out.reshape(n, 2, ng, LANES, w_out.shape[-1]).transpose(0, 2, 1, 3, 4).reshape(w_out.shape)


def kernel(x, ev_w_in, ev_conv_a, ev_conv_b_w, ev_conv_b_b, ev_w_r, ev_b_r, ev_w_i, ev_b_i, ev_lam, ev_w_out, od_w_qkv, od_rel_bias, od_w_o, pk_w_q, pk_sub_keys, pk_u, pk_v, ln1_g, ln1_b, ln2_g, ln2_b):
    bsz, seq, d = x.shape
    w_in = ev_w_in.astype(BF16)
    w_out = _interleave_groups(ev_w_out).astype(BF16)
    w_qkv = od_w_qkv.astype(BF16)
    w_o = od_w_o.astype(BF16)
    w_q = pk_w_q.astype(BF16)
    sub_keys = pk_sub_keys.astype(BF16)
    u = pk_u.astype(BF16)
    vt = _transpose_cast(pk_v)
    outs = []
    for bi in range(bsz):
        xf = x[bi]
        xb = xf.astype(BF16)
        for l in range(DEPTH):
            i = l // 2
            if l % 2 == 0:
                z = _matmul(xb, w_in, i, BF16)
                y = _mixer(z, ev_conv_a[i], ev_conv_b_w[i], ev_conv_b_b[i], ev_w_r[i], ev_b_r[i],
                           ev_w_i[i], ev_b_i[i], ev_lam[i])
                w_mix = w_out
            else:
                qkv = _matmul(xb, w_qkv, i, BF16)
                y = _neighbourhood_attention(qkv, od_rel_bias[i])
                w_mix = w_o
            xf, xb = _matmul_res_ln(y, w_mix, i, xf, ln1_g[l], ln1_b[l])
            sel = _peer_select(xb, w_q, sub_keys, l)
            xf, xb = _peer_experts(xf, u, vt, l, sel, ln2_g[l], ln2_b[l])
        outs.append(xf)
    return jnp.stack(outs)
```

```python
import math

import numpy as np
import jax
import jax.numpy as jnp
from jax import lax
from jax.experimental import pallas as pl
from jax.experimental.pallas import tpu as pltpu

F32 = jnp.float32
BF16 = jnp.bfloat16

D_MODEL = 2048
DEPTH = 4
GRID_W = 64
D_CONV = D_MODEL // 2
D_LRU = D_MODEL // 2
LRU_HEADS = 8
LRU_HD = D_LRU // LRU_HEADS
CONV_A_W = 3
CONV_B_W = 4
RG_C = 8.0
NA_HEADS = 16
NA_HD = D_MODEL // NA_HEADS
WIN_ROWS = 8
WIN_COLS = 16
PEER_HEADS = 8
PEER_QDIM = 256
N_KEYS = 128
PEER_TOPK = 16
ALPHA = (2 * DEPTH) ** 0.25
LN_EPS = 1e-5

LANES = 128
VMEM_LIMIT = 56 * 1024 * 1024
NEG_INF = float("-inf")
MASK_VALUE = -1e30

NT_DIMS = (((1,), (1,)), ((), ()))


def _params(*semantics):
    return pltpu.CompilerParams(dimension_semantics=semantics, vmem_limit_bytes=VMEM_LIMIT)


def _layer_norm(y, g, b):
    mu = jnp.mean(y, axis=-1, keepdims=True)
    yc = y - mu
    var = jnp.mean(yc * yc, axis=-1, keepdims=True)
    return yc * lax.rsqrt(var + LN_EPS) * g + b


def _gelu(x):
    return 0.5 * x * (1.0 + lax.erf(x * (1.0 / math.sqrt(2.0))))


def _matmul_kernel(x_ref, w_ref, o_ref):
    o_ref[...] = jnp.dot(x_ref[...], w_ref[...], preferred_element_type=F32).astype(o_ref.dtype)


def _matmul(x, w_stack, layer, out_dtype, tm=1024, tn=512):
    m, k = x.shape
    n = w_stack.shape[2]
    tm = min(tm, m)
    return pl.pallas_call(
        _matmul_kernel,
        grid=(m // tm, n // tn),
        in_specs=[pl.BlockSpec((tm, k), lambda i, j: (i, 0)),
                  pl.BlockSpec((None, k, tn), lambda i, j: (layer, 0, j))],
        out_specs=pl.BlockSpec((tm, tn), lambda i, j: (i, j)),
        out_shape=jax.ShapeDtypeStruct((m, n), out_dtype),
        compiler_params=_params("parallel", "parallel"),
        name="matmul",
    )(x, w_stack)


RES_LN_SUB = 256


def _matmul_res_ln_kernel(a_ref, w_ref, x_ref, g_ref, b_ref, of_ref, ob_ref):
    for r0 in range(0, a_ref.shape[0], RES_LN_SUB):
        rows = slice(r0, r0 + RES_LN_SUB)
        h = jnp.dot(a_ref[rows, :], w_ref[...], preferred_element_type=F32)
        out = _layer_norm(ALPHA * x_ref[rows, :] + h, g_ref[...], b_ref[...])
        of_ref[rows, :] = out
        ob_ref[rows, :] = out.astype(BF16)


def _matmul_res_ln(a, w_stack, layer, x, g, b, tm=2 * RES_LN_SUB):
    m, k = a.shape
    n = w_stack.shape[2]
    return pl.pallas_call(
        _matmul_res_ln_kernel,
        grid=(m // tm,),
        in_specs=[pl.BlockSpec((tm, k), lambda i: (i, 0)),
                  pl.BlockSpec((None, k, n), lambda i: (layer, 0, 0)),
                  pl.BlockSpec((tm, n), lambda i: (i, 0)),
                  pl.BlockSpec((1, n), lambda i: (0, 0)),
                  pl.BlockSpec((1, n), lambda i: (0, 0))],
        out_specs=[pl.BlockSpec((tm, n), lambda i: (i, 0)),
                   pl.BlockSpec((tm, n), lambda i: (i, 0))],
        out_shape=[jax.ShapeDtypeStruct((m, n), F32), jax.ShapeDtypeStruct((m, n), BF16)],
        compiler_params=_params("parallel"),
        name="matmul_res_ln",
    )(a, w_stack, x, g.reshape(1, n), b.reshape(1, n))


MIX_TC = 256
MIX_HALO = 16


def _mixer_kernel(gb_ref, gc_ref, xa_ref, gg_ref, xb_ref, ca_ref, cbw_ref, cbb_ref,
                  wr_ref, wi_ref, br_ref, bi_ref, lam_ref, y_ref, hf_scr):
    t_total = xb_ref.shape[0]
    n_chunks = t_total // MIX_TC
    n_ext = MIX_TC + 2 * MIX_HALO
    row = lax.broadcasted_iota(jnp.int32, (MIX_TC, LANES), 0)

    def ext(ref, c):
        t0 = pl.multiple_of(c * MIX_TC, MIX_TC)
        cur = ref[pl.ds(t0, MIX_TC), :].astype(F32)
        lo = pl.multiple_of(jnp.maximum(t0 - MIX_HALO, 0), MIX_HALO)
        hi = pl.multiple_of(jnp.minimum(t0 + MIX_TC, t_total - MIX_HALO), MIX_HALO)
        prev = jnp.where(c > 0, ref[pl.ds(lo, MIX_HALO), :].astype(F32), 0.0)
        nxt = jnp.where(c < n_chunks - 1, ref[pl.ds(hi, MIX_HALO), :].astype(F32), 0.0)
        return jnp.concatenate([prev, cur, nxt], axis=0)

    def shifted(xe, d):
        return pltpu.roll(xe, n_ext - MIX_HALO - d, 0)[:MIX_TC]

    def lru_inputs(c, direction):
        xe = ext(xb_ref, c)
        xc = jnp.zeros((MIX_TC, LANES), F32) + cbb_ref[direction:direction + 1, :]
        for k in range(CONV_B_W):
            d = k if direction == 1 else k - (CONV_B_W - 1)
            xc = xc + cbw_ref[direction, k:k + 1, :] * shifted(xe, d)
        xcb = xc.astype(BF16)
        r = jax.nn.sigmoid(jnp.dot(xcb, wr_ref[direction, 0], preferred_element_type=F32)
                           + br_ref[direction:direction + 1, :])
        i = jax.nn.sigmoid(jnp.dot(xcb, wi_ref[direction, 0], preferred_element_type=F32)
                           + bi_ref[direction:direction + 1, :])
        lam = lam_ref[direction:direction + 1, :]
        softplus_neg_lam = jnp.maximum(-lam, 0.0) + jnp.log1p(jnp.exp(-jnp.abs(lam)))
        log_a = -RG_C * r * softplus_neg_lam
        a = jnp.exp(log_a)
        u = jnp.sqrt(1.0 - jnp.exp(2.0 * log_a)) * i * xc
        return a, u

    def scan_chunk(a, u, carry, reverse):
        s = 1
        while s < MIX_TC:
            if reverse:
                keep = row < MIX_TC - s
                a_sh = jnp.where(keep, pltpu.roll(a, MIX_TC - s, 0), 1.0)
                u_sh = jnp.where(keep, pltpu.roll(u, MIX_TC - s, 0), 0.0)
            else:
                keep = row >= s
                a_sh = jnp.where(keep, pltpu.roll(a, s, 0), 1.0)
                u_sh = jnp.where(keep, pltpu.roll(u, s, 0), 0.0)
            u = a * u_sh + u
            a = a * a_sh
            s *= 2
        return u + a * carry

    def forward_body(c, carry):
        t0 = pl.multiple_of(c * MIX_TC, MIX_TC)
        pe = ext(gc_ref, c) * ext(xa_ref, c)
        conv = jnp.zeros((MIX_TC, LANES), F32)
        for k in range(CONV_A_W):
            conv = conv + ca_ref[k:k + 1, :] * shifted(pe, k - CONV_A_W // 2)
        y_a = gb_ref[pl.ds(t0, MIX_TC), :].astype(F32) * conv
        y_ref[pl.ds(t0, MIX_TC), 0:LANES] = y_a.astype(BF16)
        a, u = lru_inputs(c, 0)
        h = scan_chunk(a, u, carry, reverse=False)
        hf_scr[pl.ds(t0, MIX_TC), :] = h
        return h[MIX_TC - 1:MIX_TC, :]

    def backward_body(j, carry):
        c = n_chunks - 1 - j
        t0 = pl.multiple_of(c * MIX_TC, MIX_TC)
        a, u = lru_inputs(c, 1)
        h = scan_chunk(a, u, carry, reverse=True)
        gate = _gelu(gg_ref[pl.ds(t0, MIX_TC), :].astype(F32))
        y_b = gate * (hf_scr[pl.ds(t0, MIX_TC), :] + h)
        y_ref[pl.ds(t0, MIX_TC), LANES:2 * LANES] = y_b.astype(BF16)
        return h[0:1, :]

    zero = jnp.zeros((1, LANES), F32)
    lax.fori_loop(0, n_chunks, forward_body, zero)
    lax.fori_loop(0, n_chunks, backward_body, zero)


def _mixer(z, conv_a, conv_b_w, conv_b_b, w_r, b_r, w_i, b_i, lam):
    t = z.shape[0]
    ng = D_CONV // LANES

    def zcol(part):
        return pl.BlockSpec((t, LANES), lambda g, part=part: (0, part * ng + g))

    def vec(rows):
        return pl.BlockSpec((rows, LANES), lambda g: (0, g))

    gate_w = pl.BlockSpec((2, 1, LRU_HD, LRU_HD), lambda g: (0, g, 0, 0))
    return pl.pallas_call(
        _mixer_kernel,
        grid=(ng,),
        in_specs=[zcol(0), zcol(1), zcol(2), zcol(3), zcol(4),
                  vec(CONV_A_W),
                  pl.BlockSpec((2, CONV_B_W, LANES), lambda g: (0, 0, g)),
                  vec(2), gate_w, gate_w, vec(2), vec(2), vec(2)],
        out_specs=pl.BlockSpec((t, 2 * LANES), lambda g: (0, g)),
        out_shape=jax.ShapeDtypeStruct((t, D_MODEL), BF16),
        scratch_shapes=[pltpu.VMEM((t, LANES), F32)],
        compiler_params=_params("parallel"),
        name="mixer",
    )(z, z, z, z, z, conv_a, conv_b_w, conv_b_b, w_r.astype(BF16), w_i.astype(BF16), b_r, b_i, lam)


NA_QROWS = 8
NA_KROWS = 16
NA_HPS = 4


def _na_block_plan(rows, r0):
    kh = min(WIN_ROWS, rows)
    kr0 = min(max(r0 - WIN_ROWS // 2, 0), rows - NA_KROWS)
    plan = []
    for qi in range(NA_QROWS):
        qr = r0 + qi
        row_start = min(max(qr - kh // 2, 0), rows - kh)
        for pair in range(NA_KROWS // 2):
            kr = (kr0 + 2 * pair, kr0 + 2 * pair + 1)
            ok = [row_start <= r < row_start + kh for r in kr]
            d = [r - qr + (WIN_ROWS - 1) for r in kr]
            if ok[0] and ok[1]:
                plan.append((qi, pair, 0, d[0]))
            elif ok[0]:
                plan.append((qi, pair, 1, d[0]))
            elif ok[1]:
                plan.append((qi, pair, 2, d[1]))
    return plan


def _na_kernel(q_ref, k_ref, v_ref, tab_ref, o_ref, bias_scr):
    rows = k_ref.shape[0] // GRID_W
    nrb = rows // NA_QROWS
    rb = pl.program_id(1)

    def build(r0):
        bias_scr[...] = jnp.full(bias_scr.shape, MASK_VALUE, F32)
        for hh in range(NA_HPS):
            for qi, pair, kind, d in _na_block_plan(rows, r0):
                bias_scr[hh, qi * GRID_W:(qi + 1) * GRID_W, pair * LANES:(pair + 1) * LANES] = tab_ref[hh, kind, d]

    pl.when(rb == 0)(lambda: build(0))
    if nrb > 2:
        pl.when(rb == 1)(lambda: build(NA_QROWS))
    pl.when(rb == nrb - 1)(lambda: build(rows - NA_QROWS))

    kr0 = jnp.clip(rb * NA_QROWS - WIN_ROWS // 2, 0, rows - NA_KROWS)
    k0 = pl.multiple_of(kr0 * GRID_W, (WIN_ROWS // 2) * GRID_W)
    for hh in range(NA_HPS):
        cols = slice(hh * NA_HD, (hh + 1) * NA_HD)
        k = k_ref[pl.ds(k0, NA_KROWS * GRID_W), cols]
        v = v_ref[pl.ds(k0, NA_KROWS * GRID_W), cols]
        s = lax.dot_general(q_ref[:, cols], k, NT_DIMS, preferred_element_type=F32)
        s = s * (NA_HD ** -0.5) + bias_scr[hh]
        m = jnp.max(s, axis=-1, keepdims=True)
        p = jnp.exp(s - m)
        l = jnp.sum(p, axis=-1, keepdims=True)
        o = jnp.dot(p.astype(BF16), v, preferred_element_type=F32)
        o_ref[:, cols] = (o / l).astype(o_ref.dtype)


def _na_bias_table(rel_bias):
    c = np.arange(GRID_W)
    col_start = np.clip(c - WIN_COLS // 2, 0, GRID_W - WIN_COLS)
    col_ok = (c[None, :] >= col_start[:, None]) & (c[None, :] < col_start[:, None] + WIN_COLS)
    coff = np.clip(c[None, :] - c[:, None], -(WIN_COLS - 1), WIN_COLS - 1) + (WIN_COLS - 1)
    n_coff = 2 * WIN_COLS - 1
    col_onehot = (coff[None] == np.arange(n_coff)[:, None, None]).astype(np.float32)
    by_col = jnp.einsum("hdc,cqk->hdqk", rel_bias, col_onehot, precision=lax.Precision.HIGHEST)
    e = jnp.where(col_ok[None, None], by_col, MASK_VALUE)
    masked = jnp.full_like(e, MASK_VALUE)
    e_next = jnp.concatenate([e[:, 1:], masked[:, :1]], axis=1)
    return jnp.stack([jnp.concatenate([e, e_next], axis=-1),
                      jnp.concatenate([e, masked], axis=-1),
                      jnp.concatenate([masked, e], axis=-1)], axis=1)


def _neighbourhood_attention(qkv, rel_bias):
    t = qkv.shape[0]
    rows = t // GRID_W
    nrb = rows // NA_QROWS
    tq = NA_QROWS * GRID_W
    assert 2 * GRID_W == LANES and rows >= NA_KROWS
    table = _na_bias_table(rel_bias)
    ng = NA_HEADS // NA_HPS
    wide = NA_HPS * NA_HD
    resident = dict(pipeline_mode=pl.Buffered(1))
    return pl.pallas_call(
        _na_kernel,
        grid=(ng, nrb),
        in_specs=[pl.BlockSpec((tq, wide), lambda g, rb: (rb, g)),
                  pl.BlockSpec((t, wide), lambda g, rb: (0, ng + g), **resident),
                  pl.BlockSpec((t, wide), lambda g, rb: (0, 2 * ng + g), **resident),
                  pl.BlockSpec((NA_HPS,) + table.shape[1:], lambda g, rb: (g, 0, 0, 0, 0), **resident)],
        out_specs=pl.BlockSpec((tq, wide), lambda g, rb: (rb, g)),
        out_shape=jax.ShapeDtypeStruct((t, D_MODEL), BF16),
        scratch_shapes=[pltpu.VMEM((NA_HPS, tq, NA_KROWS * GRID_W), F32)],
        compiler_params=_params("parallel", "arbitrary"),
        name="neighbourhood_attention",
    )(qkv, qkv, qkv, table)


SEL_TT = 512
N_TOP = PEER_TOPK + 1
CAND_PAIRS = [(p, q) for p in range(N_TOP) for q in range(N_TOP // (p + 1))]
N_CAND_ROWS = -(-len(CAND_PAIRS) // 8) * 8


def _top_values(arrays, n, ranked):
    rows = [[] for _ in arrays]
    prev = [jnp.full((1, s.shape[1]), jnp.inf, F32) for s in arrays]
    rank = [jnp.zeros_like(s) if ranked[k] else None for k, s in enumerate(arrays)]
    for r in range(n):
        for k, s in enumerate(arrays):
            below = s < prev[k]
            if ranked[k] and r > 0:
                rank[k] = jnp.where(below, float(r), rank[k])
            prev[k] = jnp.max(jnp.where(below, s, NEG_INF), axis=0, keepdims=True)
            rows[k].append(prev[k])
    rank = [jnp.where(s < prev[k], float(n), rank[k]) if ranked[k] else None for k, s in enumerate(arrays)]
    return rows, rank


def _peer_select_kernel(xb_ref, wq_ref, keys_ref, cnt_ref, e1_ref, rank_ref, e2_ref, cand_scr):
    cand_scr[...] = jnp.full(cand_scr.shape, NEG_INF, F32)
    half = PEER_QDIM // 2

    def scores(pair):
        c0 = 2 * pair * PEER_QDIM
        q = jnp.dot(xb_ref[...], wq_ref[:, c0:c0 + 2 * PEER_QDIM], preferred_element_type=F32).astype(BF16)
        return [lax.dot_general(keys_ref[k % 2], q[:, k * half:(k + 1) * half], NT_DIMS,
                                preferred_element_type=F32) for k in range(4)]

    upcoming = scores(0)
    for h in range(PEER_HEADS):
        if h % 2 == 0:
            current = upcoming
            if h + 2 < PEER_HEADS:
                upcoming = scores(h // 2 + 1)
        s1, s2 = current[2 * (h % 2)], current[2 * (h % 2) + 1]
        (a, b), (_, rank) = _top_values([s1, s2], N_TOP, (False, True))
        for n, (p, q_) in enumerate(CAND_PAIRS):
            cand_scr[n:n + 1, :] = a[p] + b[q_]
        cand = cand_scr[...]
        (c,), _ = _top_values([cand], N_TOP, (False,))
        c_last, c_next = c[PEER_TOPK - 1], c[PEER_TOPK]
        thr = 0.5 * (c_last + c_next)
        m = a[0] + b[0]
        z = jnp.sum(jnp.where(cand >= c_last, jnp.exp(cand - m), 0.0), axis=0, keepdims=True)
        need = thr - s1
        cnt = jnp.zeros_like(s1)
        for q_ in range(N_TOP):
            cnt = jnp.where(b[q_] >= need, q_ + 1.0, cnt)
        cnt_ref[h] = cnt
        e1_ref[h] = jnp.exp(s1 - a[0]) / z
        rank_ref[h] = rank.astype(BF16)
        e2_ref[h] = jnp.exp(s2 - b[0]).astype(BF16)


def _peer_select(xb, w_q_stack, sub_keys_stack, layer):
    t, d = xb.shape
    shape = (PEER_HEADS, N_KEYS, t)
    out_spec = pl.BlockSpec((PEER_HEADS, N_KEYS, SEL_TT), lambda i: (0, 0, i))
    return pl.pallas_call(
        _peer_select_kernel,
        grid=(t // SEL_TT,),
        in_specs=[pl.BlockSpec((SEL_TT, d), lambda i: (i, 0)),
                  pl.BlockSpec((None, d, PEER_HEADS * PEER_QDIM), lambda i: (layer, 0, 0)),
                  pl.BlockSpec((None, 2, N_KEYS, PEER_QDIM // 2), lambda i: (layer, 0, 0, 0))],
        out_specs=[out_spec] * 4,
        out_shape=[jax.ShapeDtypeStruct(shape, F32), jax.ShapeDtypeStruct(shape, F32),
                   jax.ShapeDtypeStruct(shape, BF16), jax.ShapeDtypeStruct(shape, BF16)],
        scratch_shapes=[pltpu.VMEM((N_CAND_ROWS, SEL_TT), F32)],
        compiler_params=_params("parallel"),
        name="peer_select",
    )(xb, w_q_stack, sub_keys_stack)


PEER_TT = 512
PEER_TE = 1024
PEER_C1 = 256
PEER_C2 = 512
PEER_D2 = 512
SUB16 = 16
PEER_VMEM_LIMIT = 63 * 1024 * 1024
assert PEER_TE == 4 * PEER_C1 == 2 * PEER_C2 and D_MODEL == 4 * PEER_D2


def _gated_activation(h_blk, i, cnt_ref, e1_ref, rank_ref, e2_ref, a_scr, row0):
    tt = h_blk.shape[1]
    n_jb = N_KEYS // SUB16
    gates = [None] * n_jb
    for h in range(PEER_HEADS):
        cnt_i = jnp.broadcast_to(cnt_ref[h, pl.ds(i, 1), :], (SUB16, tt)).astype(BF16)
        e1_i = jnp.broadcast_to(e1_ref[h, pl.ds(i, 1), :], (SUB16, tt)).astype(BF16)
        for jb in range(n_jb):
            rows = slice(jb * SUB16, (jb + 1) * SUB16)
            term = jnp.where(rank_ref[h, rows, :] < cnt_i, e2_ref[h, rows, :], 0) * e1_i
            gates[jb] = term if h == 0 else gates[jb] + term
    for jb in range(n_jb):
        rows = slice(jb * SUB16, (jb + 1) * SUB16)
        act = _gelu(h_blk[rows])
        a_scr[row0 + jb * SUB16:row0 + (jb + 1) * SUB16, :] = act.astype(BF16) * gates[jb]


def _peer_expert_kernel(x_ref, u_ref, vt_ref, cnt_ref, e1_ref, rank_ref, e2_ref, g_ref, beta_ref,
                        of_ref, ob_ref, xb_scr, acc_scr, a_scr):
    e = pl.program_id(1)
    n_i = PEER_TE // N_KEYS

    @pl.when(e == 0)
    def _():
        xb_scr[...] = x_ref[...].astype(BF16)
        acc_scr[...] = jnp.zeros(acc_scr.shape, F32)

    xb = xb_scr[...]

    def first_matmul(c):
        r0 = c * PEER_C1
        return lax.dot_general(u_ref[r0:r0 + PEER_C1, :], xb, NT_DIMS, preferred_element_type=F32)

    def activation(ii, h_t):
        r = (ii * N_KEYS) % PEER_C1
        _gated_activation(h_t[r:r + N_KEYS], e * n_i + ii, cnt_ref, e1_ref, rank_ref, e2_ref, a_scr, ii * N_KEYS)

    def second_matmul(c2, piece):
        r0, k0 = piece * PEER_D2, c2 * PEER_C2
        acc_scr[r0:r0 + PEER_D2, :] += jnp.dot(vt_ref[r0:r0 + PEER_D2, k0:k0 + PEER_C2], a_scr[k0:k0 + PEER_C2, :],
                                               preferred_element_type=F32)

    h0 = first_matmul(0)
    h1 = first_matmul(1)
    activation(0, h0)
    activation(1, h0)
    h2 = first_matmul(2)
    activation(2, h1)
    activation(3, h1)
    h3 = first_matmul(3)
    activation(4, h2)
    second_matmul(0, 0)
    activation(5, h2)
    second_matmul(0, 1)
    activation(6, h3)
    second_matmul(0, 2)
    activation(7, h3)
    second_matmul(0, 3)
    for piece in range(acc_scr.shape[0] // PEER_D2):
        second_matmul(1, piece)

    @pl.when(e == pl.num_programs(1) - 1)
    def _():
        y = ALPHA * x_ref[...] + acc_scr[...].T
        out = _layer_norm(y, g_ref[...], beta_ref[...])
        of_ref[...] = out
        ob_ref[...] = out.astype(BF16)


def _peer_experts(x, u_stack, vt_stack, layer, sel, g, b):
    t, d = x.shape
    n_e = u_stack.shape[1]
    sel_spec = pl.BlockSpec((PEER_HEADS, N_KEYS, PEER_TT), lambda i, e: (0, 0, i))
    return pl.pallas_call(
        _peer_expert_kernel,
        grid=(t // PEER_TT, n_e // PEER_TE),
        in_specs=[pl.BlockSpec((PEER_TT, d), lambda i, e: (i, 0)),
                  pl.BlockSpec((None, PEER_TE, d), lambda i, e: (layer, e, 0)),
                  pl.BlockSpec((None, d, PEER_TE), lambda i, e: (layer, 0, e)),
                  sel_spec, sel_spec, sel_spec, sel_spec,
                  pl.BlockSpec((1, d), lambda i, e: (0, 0)),
                  pl.BlockSpec((1, d), lambda i, e: (0, 0))],
        out_specs=[pl.BlockSpec((PEER_TT, d), lambda i, e: (i, 0)),
                   pl.BlockSpec((PEER_TT, d), lambda i, e: (i, 0))],
        out_shape=[jax.ShapeDtypeStruct((t, d), F32), jax.ShapeDtypeStruct((t, d), BF16)],
        scratch_shapes=[pltpu.VMEM((PEER_TT, d), BF16),
                        pltpu.VMEM((d, PEER_TT), F32),
                        pltpu.VMEM((PEER_TE, PEER_TT), BF16)],
        compiler_params=pltpu.CompilerParams(dimension_semantics=("parallel", "arbitrary"),
                                             vmem_limit_bytes=PEER_VMEM_LIMIT),
        name="peer_experts",
    )(x, u_stack, vt_stack, *sel, g.reshape(1, d), b.reshape(1, d))


TRANSPOSE_TE = 1024


def _transpose_cast_kernel(x_ref, o_ref):
    o_ref[...] = x_ref[...].T.astype(o_ref.dtype)


def _transpose_cast(x):
    n, e, d = x.shape
    return pl.pallas_call(
        _transpose_cast_kernel,
        grid=(n, e // TRANSPOSE_TE),
        in_specs=[pl.BlockSpec((None, TRANSPOSE_TE, d), lambda l, j: (l, j, 0))],
        out_specs=pl.BlockSpec((None, d, TRANSPOSE_TE), lambda l, j: (l, 0, j)),
        out_shape=jax.ShapeDtypeStruct((n, d, e), BF16),
        compiler_params=_params("parallel", "parallel"),
        name="transpose_cast",
    )(x)


def _interleave_groups(w_out):
    ng = D_CONV // LANES
    n = w_out.shape[0]
    return w_out.reshape(n, 2, ng, LANES, w_out.shape[-1]).transpose(0, 2, 1, 3, 4).reshape(w_out.shape)


def kernel(x, ev_w_in, ev_conv_a, ev_conv_b_w, ev_conv_b_b, ev_w_r, ev_b_r, ev_w_i, ev_b_i, ev_lam, ev_w_out, od_w_qkv, od_rel_bias, od_w_o, pk_w_q, pk_sub_keys, pk_u, pk_v, ln1_g, ln1_b, ln2_g, ln2_b):
    bsz, seq, d = x.shape
    w_in = ev_w_in.astype(BF16)
    w_out = _interleave_groups(ev_w_out).astype(BF16)
    w_qkv = od_w_qkv.astype(BF16)
    w_o = od_w_o.astype(BF16)
    w_q = pk_w_q.astype(BF16)
    sub_keys = pk_sub_keys.astype(BF16)
    u = pk_u.astype(BF16)
    vt = _transpose_cast(pk_v)
    outs = []
    for bi in range(bsz):
        xf = x[bi]
        xb = xf.astype(BF16)
        for l in range(DEPTH):
            i = l // 2
            if l % 2 == 0:
                z = _matmul(xb, w_in, i, BF16)
                y = _mixer(z, ev_conv_a[i], ev_conv_b_w[i], ev_conv_b_b[i], ev_w_r[i], ev_b_r[i],
                           ev_w_i[i], ev_b_i[i], ev_lam[i])
                w_mix = w_out
            else:
                qkv = _matmul(xb, w_qkv, i, BF16)
                y = _neighbourhood_attention(qkv, od_rel_bias[i])
                w_mix = w_o
            xf, xb = _matmul_res_ln(y, w_mix, i, xf, ln1_g[l], ln1_b[l])
            sel = _peer_select(xb, w_q, sub_keys, l)
            xf, xb = _peer_experts(xf, u, vt, l, sel, ln2_g[l], ln2_b[l])
        outs.append(xf)
    return jnp.stack(outs)
```
